```python
import math
import jax, jax.numpy as jnp
from jax import lax
import numpy as np

D_MODEL = 1024
BATCH = 16
SEQ = 256
DEPTH = 2
DEC_BATCH = 4
DEC_SEQ = 4096
PAST_LEN = 512

GRID_W = 64
HEAD_DIM = 64
ATT_W = D_MODEL // 2
N_HEADS = ATT_W // HEAD_DIM
N_KV_HEADS = 2
GQA_GROUP = N_HEADS // N_KV_HEADS
KV_W = N_KV_HEADS * HEAD_DIM
WINDOW = 128
BLOCK = 128
ROPE_BASE = 10000.0
ROPE_PAIRS = HEAD_DIM // 4
CONV_W = D_MODEL // 4
CONV_K = 3
LRU_W = D_MODEL // 4
LRU_BLOCKS = 4
LRU_BLOCK = LRU_W // LRU_BLOCKS
LRU_CONV_K = 4
LRU_C = 8.0
D_MIX = CONV_W + ATT_W + LRU_W
D_IN = 3 * CONV_W + ATT_W + 2 * KV_W + 2 * LRU_W
DENSE_FF = 2816
N_EXPERTS = 8
TOP_K = 2
EXPERT_FF = 3584
N_DENSE = (DEPTH + 1) // 2
N_MOE = DEPTH // 2
EPS = 1e-6
NEG = -1e30

kernel_name = 'hybrid_prefix_diffusion_trunk'

F32 = jnp.float32


def _rms_norm(x, g):
    xf = x.astype(F32)
    y = xf * lax.rsqrt(jnp.mean(xf * xf, axis=-1, keepdims=True) + EPS)
    return (y * g.astype(F32)).astype(x.dtype)


def _modulation(cond, w_mod, b_mod):
    m = jax.nn.silu(cond) @ w_mod + b_mod
    return jnp.split(m, 6, axis=-1)


def _modulate(h, shift, scale):
    return h * (1 + scale) + shift


def _split_proj(h, w_in):
    p = h @ w_in
    cuts = [CONV_W, 2 * CONV_W, 3 * CONV_W, 3 * CONV_W + ATT_W,
            3 * CONV_W + ATT_W + KV_W, 3 * CONV_W + ATT_W + 2 * KV_W,
            3 * CONV_W + ATT_W + 2 * KV_W + LRU_W]
    return jnp.split(p, cuts, axis=-1)


def _dwconv(x, w, pad_l, pad_r):
    T = x.shape[1]
    xp = jnp.pad(x, ((0, 0), (pad_l, pad_r), (0, 0)))
    out = xp[:, 0:T] * w[0]
    for k in range(1, w.shape[0]):
        out = out + xp[:, k:k + T] * w[k]
    return out


def _axial_rope_tables(T):
    rows = T // GRID_W
    row = jnp.repeat(jnp.arange(rows), GRID_W).astype(F32)
    col = jnp.tile(jnp.arange(GRID_W), rows).astype(F32)
    inv = 1.0 / (ROPE_BASE ** (jnp.arange(ROPE_PAIRS, dtype=F32) / ROPE_PAIRS))
    ar = row[:, None] * inv
    ac = col[:, None] * inv
    return jnp.cos(ar), jnp.sin(ar), jnp.cos(ac), jnp.sin(ac)


def _rotate(x, cos, sin):
    x1, x2 = jnp.split(x, 2, axis=-1)
    return jnp.concatenate([x1 * cos - x2 * sin, x2 * cos + x1 * sin], axis=-1)


def _apply_axial_rope(x, tables):
    cr, sr, cc, sc = (t[:, None, :] for t in tables)
    xf = x.astype(F32)
    half = HEAD_DIM // 2
    out = jnp.concatenate([_rotate(xf[..., :half], cr, sr), _rotate(xf[..., half:], cc, sc)], axis=-1)
    return out.astype(x.dtype)


def _attend(q, k, v, sink, mask):
    s = jnp.einsum('bqkgd,blkd->bkgql', q, k, preferred_element_type=F32) * (HEAD_DIM ** -0.5)
    if mask is not None:
        s = jnp.where(mask, s, NEG)
    sink_col = jnp.broadcast_to(sink.astype(F32)[None, :, :, None, None], s.shape[:-1] + (1,))
    p = jax.nn.softmax(jnp.concatenate([s, sink_col], axis=-1), axis=-1)[..., :-1]
    return jnp.einsum('bkgql,blkd->bqkgd', p.astype(v.dtype), v)


def _context_attention(q, k, v, sink):
    B, T = q.shape[:2]
    nb = T // BLOCK
    qb = q.reshape(B, nb, BLOCK, N_KV_HEADS, GQA_GROUP, HEAD_DIM).swapaxes(0, 1)
    ob = lax.map(lambda qi: _attend(qi, k, v, sink, None), qb)
    return ob.swapaxes(0, 1).reshape(B, T, ATT_W)


def _latent_attention(q, k, v, k_ctx, v_ctx, sink):
    B, T = q.shape[:2]
    nb = T // BLOCK
    P = k_ctx.shape[1]
    kp = jnp.pad(k, ((0, 0), (BLOCK, BLOCK), (0, 0), (0, 0)))
    vp = jnp.pad(v, ((0, 0), (BLOCK, BLOCK), (0, 0), (0, 0)))
    qb = q.reshape(B, nb, BLOCK, N_KV_HEADS, GQA_GROUP, HEAD_DIM).swapaxes(0, 1)
    ctx_mask = jnp.ones((BLOCK, P), dtype=bool)

    def blk(args):
        j, qi = args
        start = j * BLOCK
        kw = lax.dynamic_slice_in_dim(kp, start, 3 * BLOCK, axis=1)
        vw = lax.dynamic_slice_in_dim(vp, start, 3 * BLOCK, axis=1)
        qpos = start + jnp.arange(BLOCK)
        kpos = start - BLOCK + jnp.arange(3 * BLOCK)
        band = (jnp.abs(qpos[:, None] - kpos[None, :]) <= WINDOW) & (kpos[None, :] >= 0) & (kpos[None, :] < T)
        k_all = jnp.concatenate([kw, k_ctx], axis=1)
        v_all = jnp.concatenate([vw, v_ctx], axis=1)
        m_all = jnp.concatenate([band, ctx_mask], axis=1)
        return _attend(qi, k_all, v_all, sink, m_all)

    ob = lax.map(blk, (jnp.arange(nb), qb))
    return ob.swapaxes(0, 1).reshape(B, T, ATT_W)


def _blockdiag(x, w):
    xb = x.reshape(x.shape[:-1] + (LRU_BLOCKS, LRU_BLOCK))
    return jnp.einsum('btnd,nde->btne', xb, w.astype(x.dtype)).reshape(x.shape)


def _rglru_scan(x, h0, wa, ba, wx, bx, lam, reverse):
    xf = x.astype(F32)
    r = jax.nn.sigmoid(_blockdiag(xf, wa) + ba.astype(F32))
    i = jax.nn.sigmoid(_blockdiag(xf, wx) + bx.astype(F32))
    log_a = LRU_C * r * jax.nn.log_sigmoid(lam.astype(F32))
    a = jnp.exp(log_a)
    u = jnp.sqrt(-jnp.expm1(2.0 * log_a)) * (i * xf)

    def step(h, au):
        a_t, u_t = au
        h = a_t * h + u_t
        return h, h

    hT, hs = lax.scan(step, h0, (a.swapaxes(0, 1), u.swapaxes(0, 1)), reverse=reverse)
    return hs.swapaxes(0, 1), hT


def _rglru_mixer(lx, lg, h0, conv_w, conv_b, wa, ba, wx, bx, lam):
    xf = _dwconv(lx, conv_w[0], LRU_CONV_K - 1, 0) + conv_b[0]
    xb = _dwconv(lx, conv_w[1], 0, LRU_CONV_K - 1) + conv_b[1]
    yf, hf = _rglru_scan(xf, h0[:, 0], wa[0], ba[0], wx[0], bx[0], lam[0], False)
    yb, hb = _rglru_scan(xb, h0[:, 1], wa[1], ba[1], wx[1], bx[1], lam[1], True)
    y = ((yf + yb) * jax.nn.gelu(lg.astype(F32))).astype(lx.dtype)
    return y, jnp.stack([hf, hb], axis=1)


def _context_mixer(h, w_in, conv_w, sink, lru_p, w_out):
    B, T, _ = h.shape
    cb, cc, cx, q, k, v, lx, lg = _split_proj(h, w_in)
    y_conv = cb * _dwconv(cc * cx, conv_w, 1, 1)
    q = q.reshape(B, T, N_KV_HEADS, GQA_GROUP, HEAD_DIM)
    k = k.reshape(B, T, N_KV_HEADS, HEAD_DIM)
    v = v.reshape(B, T, N_KV_HEADS, HEAD_DIM)
    y_att = _context_attention(q, k, v, sink.reshape(N_KV_HEADS, GQA_GROUP))
    h0 = jnp.zeros((B, 2, LRU_W), F32)
    y_lru, hT = _rglru_mixer(lx, lg, h0, *lru_p)
    out = jnp.concatenate([y_conv, y_att, y_lru], axis=-1) @ w_out
    return out, k, v, hT.astype(h.dtype)


def _latent_mixer(h, w_in, conv_w, sink, lru_p, w_out, k_ctx, v_ctx, h_ctx):
    B, T, _ = h.shape
    cb, cc, cx, q, k, v, lx, lg = _split_proj(h, w_in)
    y_conv = cb * _dwconv(cc * cx, conv_w, 1, 1)
    tables = _axial_rope_tables(T)
    q = _apply_axial_rope(q.reshape(B, T, N_HEADS, HEAD_DIM), tables)
    q = q.reshape(B, T, N_KV_HEADS, GQA_GROUP, HEAD_DIM)
    k = _apply_axial_rope(k.reshape(B, T, N_KV_HEADS, HEAD_DIM), tables)
    v = v.reshape(B, T, N_KV_HEADS, HEAD_DIM)
    y_att = _latent_attention(q, k, v, k_ctx.astype(k.dtype), v_ctx.astype(v.dtype),
                              sink.reshape(N_KV_HEADS, GQA_GROUP))
    y_lru, _ = _rglru_mixer(lx, lg, h_ctx.astype(F32), *lru_p)
    return jnp.concatenate([y_conv, y_att, y_lru], axis=-1) @ w_out


def _swiglu(h, w1, w3, w2):
    return (jax.nn.silu(h @ w1) * (h @ w3)) @ w2


def _moe(h, router_w, w1, w3, w2):
    shp = h.shape
    hf = h.reshape(-1, shp[-1])
    logits = (hf @ router_w).astype(F32)
    topv, topi = lax.top_k(logits, TOP_K)
    gates = jax.nn.softmax(topv, axis=-1)
    combine = jnp.sum(jax.nn.one_hot(topi, N_EXPERTS, dtype=F32) * gates[..., None], axis=1)
    out = jnp.zeros_like(hf)
    for e in range(N_EXPERTS):
        out = out + combine[:, e:e + 1].astype(hf.dtype) * _swiglu(hf, w1[e], w3[e], w2[e])
    return out.reshape(shp)


def _channel_mixer(h, l, ffn_w1, ffn_w3, ffn_w2, router_w, moe_w1, moe_w3, moe_w2):
    i = l // 2
    if l % 2 == 0:
        return _swiglu(h, ffn_w1[i], ffn_w3[i], ffn_w2[i])
    return _moe(h, router_w[i], moe_w1[i], moe_w3[i], moe_w2[i])


def setup_inputs(seed: int = 0) -> dict:
    key = jax.random.key(seed)
    ks = jax.random.split(key, 32)

    def nrm(k, shape, s):
        return jax.random.normal(k, shape, F32) * s

    D = D_MODEL
    a0 = jax.random.uniform(ks[20], (DEPTH, 2, LRU_W), F32, 0.9, 0.999) ** (1.0 / LRU_C)
    return {
        'x_prompt': nrm(ks[0], (BATCH, SEQ, D), 1.0),
        'x_sample': nrm(ks[1], (DEC_BATCH, DEC_SEQ, D), 1.0),
        'cache_k': nrm(ks[2], (DEC_BATCH, DEPTH, PAST_LEN, N_KV_HEADS, HEAD_DIM), 1.0),
        'cache_v': nrm(ks[3], (DEC_BATCH, DEPTH, PAST_LEN, N_KV_HEADS, HEAD_DIM), 1.0),
        'state_lru': nrm(ks[4], (DEC_BATCH, DEPTH, 2, LRU_W), 0.5),
        'c': nrm(ks[5], (DEC_BATCH, D), 1.0),
        'c_ctx': nrm(ks[6], (D,), 1.0),
        'norm1_g': 1.0 + nrm(ks[7], (DEPTH, D), 0.02),
        'norm2_g': 1.0 + nrm(ks[8], (DEPTH, D), 0.02),
        'w_mod': nrm(ks[9], (DEPTH, D, 6 * D), 0.5 * D ** -0.5),
        'b_mod': nrm(ks[10], (DEPTH, 6 * D), 0.02),
        'w_in': nrm(ks[11], (DEPTH, D, D_IN), D ** -0.5),
        'conv_w': nrm(ks[12], (DEPTH, CONV_K, CONV_W), CONV_K ** -0.5),
        'attn_sink': nrm(ks[13], (DEPTH, N_HEADS), 0.5),
        'lru_conv_w': nrm(ks[14], (DEPTH, 2, LRU_CONV_K, LRU_W), LRU_CONV_K ** -0.5),
        'lru_conv_b': nrm(ks[15], (DEPTH, 2, LRU_W), 0.01),
        'lru_wa': nrm(ks[16], (DEPTH, 2, LRU_BLOCKS, LRU_BLOCK, LRU_BLOCK), LRU_BLOCK ** -0.5),
        'lru_ba': nrm(ks[17], (DEPTH, 2, LRU_W), 0.01),
        'lru_wx': nrm(ks[18], (DEPTH, 2, LRU_BLOCKS, LRU_BLOCK, LRU_BLOCK), LRU_BLOCK ** -0.5),
        'lru_bx': nrm(ks[19], (DEPTH, 2, LRU_W), 0.01),
        'lru_lambda': jnp.log(a0) - jnp.log1p(-a0),
        'w_out': nrm(ks[21], (DEPTH, D_MIX, D), D_MIX ** -0.5),
        'ffn_w1': nrm(ks[22], (N_DENSE, D, DENSE_FF), D ** -0.5),
        'ffn_w3': nrm(ks[23], (N_DENSE, D, DENSE_FF), D ** -0.5),
        'ffn_w2': nrm(ks[24], (N_DENSE, DENSE_FF, D), DENSE_FF ** -0.5),
        'router_w': nrm(ks[25], (N_MOE, D, N_EXPERTS), D ** -0.5),
        'moe_w1': nrm(ks[26], (N_MOE, N_EXPERTS, D, EXPERT_FF), D ** -0.5),
        'moe_w3': nrm(ks[27], (N_MOE, N_EXPERTS, D, EXPERT_FF), D ** -0.5),
        'moe_w2': nrm(ks[28], (N_MOE, N_EXPERTS, EXPERT_FF, D), EXPERT_FF ** -0.5),
        'final_g': 1.0 + nrm(ks[29], (D,), 0.02),
    }


def reference(x_prompt, x_sample, cache_k, cache_v, state_lru, c, c_ctx, norm1_g, norm2_g,
              w_mod, b_mod, w_in, conv_w, attn_sink, lru_conv_w, lru_conv_b, lru_wa, lru_ba,
              lru_wx, lru_bx, lru_lambda, w_out, ffn_w1, ffn_w3, ffn_w2, router_w, moe_w1,
              moe_w3, moe_w2, final_g):
    x = x_prompt
    k_list, v_list, h_list = [], [], []
    for l in range(DEPTH):
        sh1, sc1, g1, sh2, sc2, g2 = _modulation(c_ctx[None, None, :], w_mod[l], b_mod[l])
        lru_p = (lru_conv_w[l], lru_conv_b[l], lru_wa[l], lru_ba[l], lru_wx[l], lru_bx[l], lru_lambda[l])
        h = _modulate(_rms_norm(x, norm1_g[l]), sh1, sc1)
        mix, k_l, v_l, h_l = _context_mixer(h, w_in[l], conv_w[l], attn_sink[l], lru_p, w_out[l])
        x = x + g1 * mix
        h = _modulate(_rms_norm(x, norm2_g[l]), sh2, sc2)
        x = x + g2 * _channel_mixer(h, l, ffn_w1, ffn_w3, ffn_w2, router_w, moe_w1, moe_w3, moe_w2)
        k_list.append(k_l)
        v_list.append(v_l)
        h_list.append(h_l)
    y_prompt = _rms_norm(x, final_g)
    new_k = jnp.stack(k_list, axis=1)
    new_v = jnp.stack(v_list, axis=1)
    new_lru = jnp.stack(h_list, axis=1)

    x = x_sample
    for l in range(DEPTH):
        sh1, sc1, g1, sh2, sc2, g2 = _modulation(c[:, None, :], w_mod[l], b_mod[l])
        lru_p = (lru_conv_w[l], lru_conv_b[l], lru_wa[l], lru_ba[l], lru_wx[l], lru_bx[l], lru_lambda[l])
        h = _modulate(_rms_norm(x, norm1_g[l]), sh1, sc1)
        mix = _latent_mixer(h, w_in[l], conv_w[l], attn_sink[l], lru_p, w_out[l],
                            cache_k[:, l], cache_v[:, l], state_lru[:, l])
        x = x + g1 * mix
        h = _modulate(_rms_norm(x, norm2_g[l]), sh2, sc2)
        x = x + g2 * _channel_mixer(h, l, ffn_w1, ffn_w3, ffn_w2, router_w, moe_w1, moe_w3, moe_w2)
    y_sample = _rms_norm(x, final_g)

    return (y_prompt, y_sample, new_k, new_v, new_lru)
```

```python
import functools

import jax
import jax.numpy as jnp
from jax import lax
from jax.experimental import pallas as pl
from jax.experimental.pallas import tpu as pltpu

F32 = jnp.float32
BF16 = jnp.bfloat16

HEAD_DIM = 64
N_KV_HEADS = 2
GQA_GROUP = 4
GRID_W = 64
ROPE_BASE = 10000.0
WINDOW_BLOCK = 128
LRU_C = 8.0
LRU_SEGMENTS = 8
EPS = 1e-6
NEG = -1e30
TOP_K = 2

ROW_TILE = 512
MOE_ROW_TILE = 512
MOE_SUPER = 2048
MOE_FF_TILE = 512
VMEM_LIMIT = 52 * 1024 * 1024


def _params(*sem):
    return pltpu.CompilerParams(dimension_semantics=sem, vmem_limit_bytes=VMEM_LIMIT)


def _mod_kernel(cond_ref, w_ref, b_ref, o_ref):
    c = cond_ref[...]
    s = c * jax.nn.sigmoid(c)
    o_ref[0] = jnp.dot(s.astype(BF16), w_ref[0].astype(BF16), preferred_element_type=F32) + b_ref[0]


def _modulation(cond, w_mod, b_mod):
    depth, d, d6 = w_mod.shape
    tn = d6 // 4
    return pl.pallas_call(
        _mod_kernel,
        grid=(depth, d6 // tn),
        in_specs=[pl.BlockSpec((8, d), lambda l, j: (0, 0)),
                  pl.BlockSpec((1, d, tn), lambda l, j: (l, 0, j)),
                  pl.BlockSpec((1, 1, tn), lambda l, j: (l, 0, j))],
        out_specs=pl.BlockSpec((1, 8, tn), lambda l, j: (l, 0, j)),
        out_shape=jax.ShapeDtypeStruct((depth, 8, d6), F32),
        compiler_params=_params("arbitrary", "arbitrary"),
        name="modulation",
    )(cond, w_mod, b_mod.reshape(depth, 1, d6))


class _Geom:
    def __init__(self, n_ctx_seq, ctx_len, n_lat_seq, lat_len):
        self.n_ctx_seq, self.ctx_len, self.n_lat_seq, self.lat_len = n_ctx_seq, ctx_len, n_lat_seq, lat_len
        self.n_ctx = n_ctx_seq * ctx_len
        self.n_lat = n_lat_seq * lat_len
        self.n_tok = self.n_ctx + self.n_lat
        assert self.n_ctx % ROW_TILE == 0 and lat_len % ROW_TILE == 0 and ROW_TILE % ctx_len == 0
        assert self.n_ctx % lat_len == 0 or self.n_ctx < lat_len
        self.ctx_tiles = self.n_ctx // ROW_TILE
        self.lat_tiles_per_seq = lat_len // ROW_TILE
        self.n_tiles = self.n_tok // ROW_TILE

    def mod_row(self, i):
        return jnp.where(i < self.ctx_tiles, 0, 1 + (i - self.ctx_tiles) // self.lat_tiles_per_seq)

    def rope_block(self, i):
        return jnp.where(i < self.ctx_tiles, 0, 1 + (i - self.ctx_tiles) % self.lat_tiles_per_seq)


def _rms_mod(x, g, shift, scale):
    y = x * lax.rsqrt(jnp.mean(x * x, axis=-1, keepdims=True) + EPS)
    return (y * g) * (1 + scale) + shift


def _rope(x, cos, sin):
    lane = lax.broadcasted_iota(jnp.int32, x.shape, 1)
    swapped = jnp.where((lane & 31) < 16, pltpu.roll(x, 128 - 16, 1), pltpu.roll(x, 16, 1))
    return x * cos + swapped * sin


def _stage_a_kernel(x_ref, mod_ref, g_ref, w_ref, cos_ref, sin_ref,
                    cbz_ref, zb_ref, q_ref, k_ref, v_ref, lru_ref, *, cw, aw, kw):
    m = mod_ref[0, 0]
    h = _rms_mod(x_ref[...], g_ref[...], m[0:1], m[1:2]).astype(BF16)
    rows = h.shape[0]

    def proj(lo, hi):
        return jnp.dot(h, w_ref[:, lo:hi], preferred_element_type=F32)

    cb = proj(0, cw)
    z = proj(cw, 2 * cw) * proj(2 * cw, 3 * cw)
    cbz_ref[:, 0:cw] = cb
    cbz_ref[:, cw:2 * cw] = z
    zb_ref[0, 0:8] = z[0:8]
    zb_ref[0, 8:16] = z[rows - 8:rows]

    cos, sin = cos_ref[...], sin_ref[...]
    o = 3 * cw
    for c in range(aw // 128):
        qc = _rope(proj(o + c * 128, o + (c + 1) * 128), cos, sin)
        q_ref[:, c * 128:(c + 1) * 128] = (qc * (HEAD_DIM ** -0.5)).astype(BF16)
    o += aw
    k_ref[...] = _rope(proj(o, o + kw), cos, sin)
    v_ref[...] = proj(o + kw, o + 2 * kw)
    o += 2 * kw
    lru_ref[...] = proj(o, o + 2 * cw)


def _stage_a(geom, x, mods, l, norm_g, w_in_bf, cos_t, sin_t, cw, aw, kw):
    n, d = x.shape
    tm = ROW_TILE
    d_in = w_in_bf.shape[1]
    row = lambda i: (i, 0)
    out_shape = (jax.ShapeDtypeStruct((n, 2 * cw), F32),
                 jax.ShapeDtypeStruct((geom.n_tiles, 16, cw), F32),
                 jax.ShapeDtypeStruct((n, aw), BF16),
                 jax.ShapeDtypeStruct((n, kw), F32),
                 jax.ShapeDtypeStruct((n, kw), F32),
                 jax.ShapeDtypeStruct((n, 2 * cw), F32))
    return pl.pallas_call(
        functools.partial(_stage_a_kernel, cw=cw, aw=aw, kw=kw),
        grid=(geom.n_tiles,),
        in_specs=[pl.BlockSpec((tm, d), row),
                  pl.BlockSpec((1, 1, 6, d), lambda i: (l, geom.mod_row(i), 0, 0)),
                  pl.BlockSpec((1, d), lambda i: (0, 0)),
                  pl.BlockSpec((d, d_in), lambda i: (0, 0)),
                  pl.BlockSpec((tm, 128), lambda i: (geom.rope_block(i), 0)),
                  pl.BlockSpec((tm, 128), lambda i: (geom.rope_block(i), 0))],
        out_specs=(pl.BlockSpec((tm, 2 * cw), row),
                   pl.BlockSpec((1, 16, cw), lambda i: (i, 0, 0)),
                   pl.BlockSpec((tm, aw), row),
                   pl.BlockSpec((tm, kw), row),
                   pl.BlockSpec((tm, kw), row),
                   pl.BlockSpec((tm, 2 * cw), row)),
        out_shape=out_shape,
        compiler_params=_params("arbitrary"),
        name=f"stage_a_l{l}",
    )(x, mods, norm_g.reshape(1, d), w_in_bf, cos_t, sin_t)


def _softmax_pv(pieces, sink_col):
    m = sink_col
    for s, _ in pieces:
        m = jnp.maximum(m, jnp.max(s, axis=-1, keepdims=True))
    denom = jnp.exp(sink_col - m)
    acc = None
    for s, v in pieces:
        p = jnp.exp(s - m)
        denom = denom + jnp.sum(p, axis=-1, keepdims=True)
        pv = jnp.dot(p.astype(BF16), v, preferred_element_type=F32)
        acc = pv if acc is None else acc + pv
    return acc / denom


def _qk(q, k):
    return lax.dot_general(q, k, (((1,), (1,)), ((), ())), preferred_element_type=F32)


def _attn_kernel(*refs, windowed, nb):
    if windowed:
        q_ref, kl_ref, km_ref, kr_ref, vl_ref, vm_ref, vr_ref, kc_ref, vc_ref, sink_ref, o_ref = refs
    else:
        q_ref, km_ref, vm_ref, sink_ref, o_ref = refs
    tq = q_ref.shape[0]
    rows = GQA_GROUP * tq
    sink_all = sink_ref[...]
    if windowed:
        j = pl.program_id(1)
        r = lax.broadcasted_iota(jnp.int32, (rows, tq), 0) & (tq - 1)
        c = lax.broadcasted_iota(jnp.int32, (rows, tq), 1)
        left_ok = jnp.logical_and(c >= r, j > 0)
        right_ok = jnp.logical_and(c <= r, j < nb - 1)
    for g in range(N_KV_HEADS):
        lo, hi = g * HEAD_DIM, (g + 1) * HEAD_DIM
        qg = jnp.concatenate(
            [q_ref[:, (g * GQA_GROUP + h) * HEAD_DIM:(g * GQA_GROUP + h + 1) * HEAD_DIM] for h in range(GQA_GROUP)],
            axis=0)
        sink_col = jnp.concatenate(
            [jnp.broadcast_to(sink_all[g * GQA_GROUP + h:g * GQA_GROUP + h + 1, 0:1], (tq, 1))
             for h in range(GQA_GROUP)], axis=0)
        km = km_ref[:, lo:hi].astype(BF16)
        vm = vm_ref[:, lo:hi].astype(BF16)
        pieces = [(_qk(qg, km), vm)]
        if windowed:
            sl = jnp.where(left_ok, _qk(qg, kl_ref[:, lo:hi].astype(BF16)), NEG)
            sr = jnp.where(right_ok, _qk(qg, kr_ref[:, lo:hi].astype(BF16)), NEG)
            pieces = [(sl, vl_ref[:, lo:hi].astype(BF16)), pieces[0], (sr, vr_ref[:, lo:hi].astype(BF16)),
                      (_qk(qg, kc_ref[0, 0, :, lo:hi].astype(BF16)), vc_ref[0, 0, :, lo:hi].astype(BF16))]
        og = _softmax_pv(pieces, sink_col).astype(BF16)
        for h in range(GQA_GROUP):
            c0 = (g * GQA_GROUP + h) * HEAD_DIM
            o_ref[:, c0:c0 + HEAD_DIM] = og[h * tq:(h + 1) * tq]


def _attention_ctx(geom, q, k, v, sink_b):
    t, aw, kw = geom.ctx_len, q.shape[1], k.shape[1]
    blk = lambda b: (b, 0)
    return pl.pallas_call(
        functools.partial(_attn_kernel, windowed=False, nb=1),
        grid=(geom.n_ctx_seq,),
        in_specs=[pl.BlockSpec((t, aw), blk), pl.BlockSpec((t, kw), blk), pl.BlockSpec((t, kw), blk),
                  pl.BlockSpec(sink_b.shape, lambda b: (0, 0))],
        out_specs=pl.BlockSpec((t, aw), blk),
        out_shape=jax.ShapeDtypeStruct((geom.n_ctx, aw), BF16),
        compiler_params=_params("arbitrary"),
        name="attention_ctx",
    )(q, k, v, sink_b)


def _attention_lat(geom, q, k, v, cache_k, cache_v, l, sink_b):
    aw, kw = q.shape[1], k.shape[1]
    tq = WINDOW_BLOCK
    nb = geom.lat_len // tq
    off = geom.n_ctx // tq
    past = cache_k.shape[2]

    def at(delta):
        return lambda b, j: (off + b * nb + jnp.clip(j + delta, 0, nb - 1), 0)

    kv_spec = [pl.BlockSpec((tq, kw), at(-1)), pl.BlockSpec((tq, kw), at(0)), pl.BlockSpec((tq, kw), at(1))]
    cache_spec = pl.BlockSpec((1, 1, past, kw), lambda b, j: (b, l, 0, 0))
    return pl.pallas_call(
        functools.partial(_attn_kernel, windowed=True, nb=nb),
        grid=(geom.n_lat_seq, nb),
        in_specs=[pl.BlockSpec((tq, aw), at(0))] + kv_spec + kv_spec + [cache_spec, cache_spec,
                  pl.BlockSpec(sink_b.shape, lambda b, j: (0, 0))],
        out_specs=pl.BlockSpec((tq, aw), lambda b, j: (b * nb + j, 0)),
        out_shape=jax.ShapeDtypeStruct((geom.n_lat, aw), BF16),
        compiler_params=_params("arbitrary", "arbitrary"),
        name="attention_lat",
    )(q, k, k, k, v, v, v, cache_k, cache_v, sink_b)


def _log_sigmoid(x):
    return jnp.minimum(x, 0.0) - jnp.log(1.0 + jnp.exp(-jnp.abs(x)))


def _lru_kernel(x_ref, h0_ref, cw_ref, cb_ref, wg_ref, bg_ref, lam_ref, y_ref, ht_ref,
                xp_ref, a_ref, uf_ref, ub_ref, hin_ref, *, t, lw):
    seg = LRU_SEGMENTS
    seg_len = t // seg
    tg = min(t, 256)
    tc = min(seg_len, 256)
    nh = lw // 128

    def put(ref, r0, rows, val):
        for hh in range(nh):
            ref[hh, pl.ds(r0, rows), :] = val[:, hh * 128:(hh + 1) * 128]

    def get(ref, r0, rows):
        return jnp.concatenate([ref[hh, pl.ds(r0, rows), :] for hh in range(nh)], axis=1)

    zeros8 = jnp.zeros((8, lw), F32)
    xp_ref[0:8] = zeros8
    xp_ref[t + 8:t + 16] = zeros8
    xp_ref[8:t + 8] = x_ref[:, 0:lw]

    for d, u_ref in ((0, uf_ref), (1, ub_ref)):
        cw = cw_ref[d]
        log_sig = _log_sigmoid(lam_ref[d])

        def gate_chunk(c, carry, d=d, u_ref=u_ref, cw=cw, log_sig=log_sig):
            r0 = pl.multiple_of(c * tg, tg)
            win = xp_ref[pl.ds(r0, tg + 16), :]
            xc = None
            for k in range(cw.shape[0]):
                s0 = 8 - (cw.shape[0] - 1) + k if d == 0 else 8 + k
                term = win[s0:s0 + tg] * cw[k:k + 1]
                xc = term if xc is None else xc + term
            xc = xc + cb_ref[d]
            gates = jnp.dot(xc.astype(BF16), wg_ref[d], preferred_element_type=F32) + bg_ref[d]
            r = jax.nn.sigmoid(gates[:, 0:lw])
            i = jax.nn.sigmoid(gates[:, lw:2 * lw])
            log_a = (LRU_C * r) * log_sig
            a = jnp.exp(log_a)
            u = jnp.sqrt(-jnp.tanh(log_a) * (a * a + 1.0)) * (i * xc)
            put(a_ref, r0, tg, a)
            put(u_ref, r0, tg, u)
            return carry

        lax.fori_loop(0, t // tg, gate_chunk, 0)

        def scan_step(jj, carry, d=d, u_ref=u_ref):
            j = jj if d == 0 else seg_len - 1 - jj
            idx = pl.ds(j, seg, stride=seg_len)
            out = []
            for hh in range(nh):
                p, u = carry[2 * hh], carry[2 * hh + 1]
                a = a_ref[hh, idx, :]
                p = a * p
                u = a * u + u_ref[hh, idx, :]
                a_ref[hh, idx, :] = p
                u_ref[hh, idx, :] = u
                out += [p, u]
            return tuple(out)

        init = (jnp.ones((seg, 128), F32), jnp.zeros((seg, 128), F32)) * nh
        ends = lax.fori_loop(0, seg_len, scan_step, init)
        p_end = jnp.concatenate(ends[0::2], axis=1)
        u_end = jnp.concatenate(ends[1::2], axis=1)

        hcur = h0_ref[0, d:d + 1, :]
        order = range(seg) if d == 0 else range(seg - 1, -1, -1)
        for s in order:
            hin_ref[d, s:s + 1, :] = hcur
            hcur = p_end[s:s + 1] * hcur + u_end[s:s + 1]
        ht_ref[0, d:d + 1, :] = hcur

        if d == 0:
            def fix_chunk(c, carry):
                r0 = pl.multiple_of(c * tc, tc)
                hin = hin_ref[0, pl.ds(r0 // seg_len, 1), :]
                put(uf_ref, r0, tc, get(a_ref, r0, tc) * hin + get(uf_ref, r0, tc))
                return carry

            lax.fori_loop(0, t // tc, fix_chunk, 0)

    def out_chunk(c, carry):
        r0 = pl.multiple_of(c * tc, tc)
        hin = hin_ref[1, pl.ds(r0 // seg_len, 1), :]
        hb = get(a_ref, r0, tc) * hin + get(ub_ref, r0, tc)
        lg = x_ref[pl.ds(r0, tc), lw:2 * lw]
        y_ref[pl.ds(r0, tc), :] = ((get(uf_ref, r0, tc) + hb) * jax.nn.gelu(lg)).astype(BF16)
        return carry

    lax.fori_loop(0, t // tc, out_chunk, 0)


def _lru(lru_in, row_off, n_seq, t, h0, lru_p):
    conv_w, conv_b, w_gate, b_gate, lam = lru_p
    lw = lru_in.shape[1] // 2
    assert row_off % t == 0 and t % (8 * LRU_SEGMENTS) == 0
    off = row_off // t
    whole = lambda a: pl.BlockSpec(a.shape, lambda b: (0,) * a.ndim)
    return pl.pallas_call(
        functools.partial(_lru_kernel, t=t, lw=lw),
        grid=(n_seq,),
        in_specs=[pl.BlockSpec((t, 2 * lw), lambda b: (off + b, 0)),
                  pl.BlockSpec((1, 2, lw), lambda b: (b, 0, 0)),
                  whole(conv_w), whole(conv_b), whole(w_gate), whole(b_gate), whole(lam)],
        out_specs=(pl.BlockSpec((t, lw), lambda b: (b, 0)),
                   pl.BlockSpec((1, 2, lw), lambda b: (b, 0, 0))),
        out_shape=(jax.ShapeDtypeStruct((n_seq * t, lw), BF16),
                   jax.ShapeDtypeStruct((n_seq, 2, lw), F32)),
        scratch_shapes=[pltpu.VMEM((t + 16, lw), F32)] + [pltpu.VMEM((lw // 128, t, 128), F32)] * 3
                       + [pltpu.VMEM((2, LRU_SEGMENTS, lw), F32)],
        compiler_params=_params("arbitrary"),
        name=f"rglru_t{t}",
    )(lru_in, h0, conv_w, conv_b, w_gate, b_gate, lam)


def _route_top2(h2b, router_ref, n_experts):
    logits = jnp.dot(h2b, router_ref[...], preferred_element_type=F32)
    lane = lax.broadcasted_iota(jnp.int32, logits.shape, 1).astype(F32)
    lg = jnp.where(lane < n_experts, logits, -jnp.inf)
    v1 = jnp.max(lg, axis=-1, keepdims=True)
    e1 = jnp.min(jnp.where(lg == v1, lane, 1e9), axis=-1, keepdims=True)
    lg2 = jnp.where(lane == e1, -jnp.inf, lg)
    v2 = jnp.max(lg2, axis=-1, keepdims=True)
    e2 = jnp.min(jnp.where(lg2 == v2, lane, 1e9), axis=-1, keepdims=True)
    ex = jnp.exp(v2 - v1)
    den = 1.0 + ex
    g1 = 1.0 / den
    g2 = ex / den
    return jnp.where(lane == 0, e1, jnp.where(lane == 1, e2, jnp.where(lane == 2, g1, jnp.where(lane == 3, g2, 0.0))))


def _pack_bf16_pair(h2b):
    n = h2b.shape[1] // 2
    lo = lax.bitcast_convert_type(h2b[:, 0:n].astype(F32), jnp.uint32)
    hi = lax.bitcast_convert_type(h2b[:, n:2 * n].astype(F32), jnp.uint32)
    return (lo >> 16) | (hi & jnp.uint32(0xFFFF0000))


def _unpack_bf16_pair(packed):
    lo = lax.bitcast_convert_type(packed << 16, F32).astype(BF16)
    hi = lax.bitcast_convert_type(packed & jnp.uint32(0xFFFF0000), F32).astype(BF16)
    return lo, hi


def _stage_c_kernel(*refs, geom, cw, aw, routed, n_experts):
    if routed:
        (x_ref, cbz_ref, zprev_ref, znext_ref, att_ref, lru_ref, mod_ref, convw_ref, wo_ref, g_ref, router_ref,
         xo_ref, h2_ref, route_ref) = refs
    else:
        (x_ref, cbz_ref, zprev_ref, znext_ref, att_ref, lru_ref, mod_ref, convw_ref, wo_ref, g_ref,
         xo_ref, h2_ref) = refs
    i = pl.program_id(0)
    tm = x_ref.shape[0]
    cb = cbz_ref[:, 0:cw]
    z = cbz_ref[:, cw:2 * cw]
    row = lax.broadcasted_iota(jnp.int32, (tm, 1), 0)
    seq_len = jnp.where(i < geom.ctx_tiles, geom.ctx_len, geom.lat_len)
    tpos = (i * tm + row) & (seq_len - 1)
    z_dn = jnp.where(row == 0, zprev_ref[0, 15:16], pltpu.roll(z, 1, 0))
    z_up = jnp.where(row == tm - 1, znext_ref[0, 0:1], pltpu.roll(z, tm - 1, 0))
    z_dn = jnp.where(tpos == 0, 0.0, z_dn)
    z_up = jnp.where(tpos == seq_len - 1, 0.0, z_up)
    cwt = convw_ref[...]
    y_conv = cb * (z_dn * cwt[0:1] + z * cwt[1:2] + z_up * cwt[2:3])

    mix = (jnp.dot(y_conv.astype(BF16), wo_ref[0:cw], preferred_element_type=F32)
           + jnp.dot(att_ref[...], wo_ref[cw:cw + aw], preferred_element_type=F32)
           + jnp.dot(lru_ref[...], wo_ref[cw + aw:2 * cw + aw], preferred_element_type=F32))
    m = mod_ref[0, 0]
    xn = x_ref[...] + m[2:3] * mix
    xo_ref[...] = xn
    h2b = _rms_mod(xn, g_ref[...], m[3:4], m[4:5]).astype(BF16)
    if routed:
        h2_ref[...] = _pack_bf16_pair(h2b)
        route_ref[...] = _route_top2(h2b, router_ref, n_experts)
    else:
        h2_ref[...] = h2b


def _stage_c(geom, x, cbz, zb, y_att, y_lru, mods, l, conv_w, w_out_bf, norm_g, router_bf, n_experts):
    n, d = x.shape
    tm = ROW_TILE
    cw, aw = conv_w.shape[1], y_att.shape[1]
    routed = router_bf is not None
    row = lambda i: (i, 0)
    last = geom.n_tiles - 1
    in_specs = [pl.BlockSpec((tm, d), row),
                pl.BlockSpec((tm, 2 * cw), row),
                pl.BlockSpec((1, 16, cw), lambda i: (jnp.maximum(i - 1, 0), 0, 0)),
                pl.BlockSpec((1, 16, cw), lambda i: (jnp.minimum(i + 1, last), 0, 0)),
                pl.BlockSpec((tm, aw), row),
                pl.BlockSpec((tm, cw), row),
                pl.BlockSpec((1, 1, 6, d), lambda i: (l, geom.mod_row(i), 0, 0)),
                pl.BlockSpec(conv_w.shape, lambda i: (0, 0)),
                pl.BlockSpec(w_out_bf.shape, lambda i: (0, 0)),
                pl.BlockSpec((1, d), lambda i: (0, 0))]
    args = [x, cbz, zb, zb, y_att, y_lru, mods, conv_w, w_out_bf, norm_g.reshape(1, d)]
    out_specs = [pl.BlockSpec((tm, d), row)]
    out_shape = [jax.ShapeDtypeStruct((n, d), F32)]
    if routed:
        in_specs.append(pl.BlockSpec(router_bf.shape, lambda i: (0, 0)))
        args.append(router_bf)
        out_specs += [pl.BlockSpec((tm, d // 2), row), pl.BlockSpec((tm, 128), row)]
        out_shape += [jax.ShapeDtypeStruct((n, d // 2), jnp.uint32), jax.ShapeDtypeStruct((n, 128), F32)]
    else:
        out_specs.append(pl.BlockSpec((tm, d), row))
        out_shape.append(jax.ShapeDtypeStruct((n, d), BF16))
    return pl.pallas_call(
        functools.partial(_stage_c_kernel, geom=geom, cw=cw, aw=aw, routed=routed, n_experts=n_experts),
        grid=(geom.n_tiles,),
        in_specs=in_specs, out_specs=tuple(out_specs), out_shape=tuple(out_shape),
        compiler_params=_params("arbitrary"),
        name=f"stage_c_l{l}",
    )(*args)


def _silu(x):
    return x * jax.nn.sigmoid(x)


def _ffn_kernel(x_ref, h_ref, mod_ref, w1_ref, w3_ref, w2_ref, o_ref, *, fc):
    h = h_ref[...]
    ff = w1_ref.shape[1]
    acc = None
    for c in range(ff // fc):
        sl = slice(c * fc, (c + 1) * fc)
        a = _silu(jnp.dot(h, w1_ref[:, sl], preferred_element_type=F32)) * jnp.dot(
            h, w3_ref[:, sl], preferred_element_type=F32)
        part = jnp.dot(a.astype(BF16), w2_ref[sl, :], preferred_element_type=F32)
        acc = part if acc is None else acc + part
    o_ref[...] = x_ref[...] + mod_ref[0, 0][5:6] * acc


def _ffn(geom, x, h2, mods, l, w1, w3, w2):
    n, d = x.shape
    tm = ROW_TILE
    ff = w1.shape[1]
    fc = 256
    assert ff % fc == 0
    row = lambda i: (i, 0)
    const = lambda a: pl.BlockSpec(a.shape, lambda i: (0, 0), pipeline_mode=pl.Buffered(1))
    return pl.pallas_call(
        functools.partial(_ffn_kernel, fc=fc),
        grid=(geom.n_tiles,),
        in_specs=[pl.BlockSpec((tm, d), row), pl.BlockSpec((tm, d), row),
                  pl.BlockSpec((1, 1, 6, d), lambda i: (l, geom.mod_row(i), 0, 0)),
                  const(w1), const(w3), const(w2)],
        out_specs=pl.BlockSpec((tm, d), row),
        out_shape=jax.ShapeDtypeStruct((n, d), F32),
        compiler_params=_params("arbitrary"),
        name=f"ffn_l{l}",
    )(x, h2, mods, w1, w3, w2)


def _moe_plan(route, n_experts):
    n = route.shape[0]
    tmg, sup = MOE_ROW_TILE, MOE_SUPER
    experts = route[:, 0:TOP_K].astype(jnp.int32)
    sel = jnp.sum(experts[:, :, None] == jnp.arange(n_experts)[None, None, :], axis=1).astype(jnp.int32)
    incl = jnp.cumsum(sel, axis=0)
    rank = incl - sel
    counts = incl[-1]
    tiles = (counts + tmg - 1) // tmg
    tile_end = jnp.cumsum(tiles)
    tile_off = tile_end - tiles
    pos = (tile_off * tmg)[experts] + jnp.take_along_axis(rank, experts, axis=1)

    n_rows = (TOP_K * n + n_experts * (tmg - 1)) // tmg * tmg
    n_rows = (n_rows + sup - 1) // sup * sup
    n_super = n_rows // sup
    per = sup // tmg
    n_work = n_super + n_experts - 1
    t_idx = jnp.arange(n_super * per)
    t_exp = jnp.minimum(jnp.searchsorted(tile_end, t_idx, side="right"), n_experts - 1).astype(jnp.int32)
    t_sup = t_idx // per
    t_live = t_sup <= (tile_end[-1] - 1) // per
    first = jnp.concatenate([jnp.ones((1,), bool),
                             (t_exp[1:] != t_exp[:-1]) | (t_sup[1:] != t_sup[:-1])]) & t_live
    w_of_t = jnp.cumsum(first.astype(jnp.int32)) - 1
    w_of_t = jnp.where(t_live, w_of_t, n_work)
    t_in = (t_idx % per).astype(jnp.int32)
    big = jnp.int32(per)
    lo = jnp.full((n_work + 1,), big, jnp.int32).at[w_of_t].min(t_in)[:n_work]
    hi = jnp.zeros((n_work + 1,), jnp.int32).at[w_of_t].max(t_in + 1)[:n_work]
    w_sup = jnp.zeros((n_work + 1,), jnp.int32).at[w_of_t].max(t_sup.astype(jnp.int32))[:n_work]
    w_exp = jnp.zeros((n_work + 1,), jnp.int32).at[w_of_t].max(t_exp)[:n_work]
    live = hi > 0
    last_sup = jnp.max(jnp.where(live, w_sup, 0))
    last_exp = jnp.max(jnp.where(live, w_exp, 0))
    w_sup = jnp.where(live, w_sup, last_sup)
    w_exp = jnp.where(live, w_exp, last_exp)
    lo = jnp.where(live, lo, 0)
    return pos.astype(jnp.int32), n_rows, (w_sup, w_exp, lo, hi)


def _moe_scatter_kernel(pos_ref, h_ref, xs_in_ref, xs_ref, sem, *, tile_off):
    del xs_in_ref
    i = pl.program_id(0)
    tm = h_ref.shape[0]
    base = (tile_off + i) * tm * TOP_K

    def copy(r, k):
        return pltpu.make_async_copy(h_ref.at[pl.ds(r, 1)], xs_ref.at[pl.ds(pos_ref[base + r * TOP_K + k], 1)], sem)

    def start(r, c):
        for k in range(TOP_K):
            copy(r, k).start()
        return c

    def wait(r, c):
        for k in range(TOP_K):
            copy(r, k).wait()
        return c

    lax.fori_loop(0, tm, start, 0)
    lax.fori_loop(0, tm, wait, 0)


def _moe_scatter(h2p, pos_flat, n_rows):
    n, w = h2p.shape
    tm = ROW_TILE
    xs0 = jnp.zeros((n_rows, w), jnp.uint32)
    return pl.pallas_call(
        functools.partial(_moe_scatter_kernel, tile_off=0),
        grid_spec=pltpu.PrefetchScalarGridSpec(
            num_scalar_prefetch=1,
            grid=(n // tm,),
            in_specs=[pl.BlockSpec((tm, w), lambda i, pos: (i, 0)),
                      pl.BlockSpec(memory_space=pl.ANY)],
            out_specs=pl.BlockSpec(memory_space=pl.ANY),
            scratch_shapes=[pltpu.SemaphoreType.DMA(())]),
        out_shape=jax.ShapeDtypeStruct((n_rows, w), jnp.uint32),
        input_output_aliases={2: 0},
        compiler_params=_params("arbitrary"),
        name="moe_scatter",
    )(pos_flat, h2p, xs0)


def _moe_gemm_kernel(wsup_ref, wexp_ref, lo_ref, hi_ref, xs_ref, w1_ref, w3_ref, w2_ref, o_ref,
                     w1b_ref, w3b_ref, w2b_ref):
    w = pl.program_id(0)
    j = pl.program_id(1)
    tmg = MOE_ROW_TILE
    lo, hi = lo_ref[w], hi_ref[w]
    half = w1b_ref.shape[0] // 2

    @pl.when(hi > lo)
    def _():
        w1b_ref[...] = w1_ref[0].astype(BF16)
        w3b_ref[...] = w3_ref[0].astype(BF16)
        w2b_ref[...] = w2_ref[0].astype(BF16)

        def sub(s, carry):
            r0 = pl.multiple_of(s * tmg, tmg)
            xa, xb = _unpack_bf16_pair(xs_ref[pl.ds(r0, tmg), :])
            h1 = (jnp.dot(xa, w1b_ref[0:half], preferred_element_type=F32)
                  + jnp.dot(xb, w1b_ref[half:2 * half], preferred_element_type=F32))
            h3 = (jnp.dot(xa, w3b_ref[0:half], preferred_element_type=F32)
                  + jnp.dot(xb, w3b_ref[half:2 * half], preferred_element_type=F32))
            part = jnp.dot((_silu(h1) * h3).astype(BF16), w2b_ref[...], preferred_element_type=F32)

            @pl.when(j == 0)
            def _():
                o_ref[pl.ds(r0, tmg), :] = part

            @pl.when(j > 0)
            def _():
                o_ref[pl.ds(r0, tmg), :] += part

            return carry

        lax.fori_loop(lo, hi, sub, 0)


def _moe_gemm(xs, plan, w1, w3, w2):
    n_rows, wp = xs.shape
    n_exp, d, ff = w1.shape
    sup, tf = MOE_SUPER, MOE_FF_TILE
    w_sup, w_exp, lo, hi = plan
    n_work = w_sup.shape[0]
    return pl.pallas_call(
        _moe_gemm_kernel,
        grid_spec=pltpu.PrefetchScalarGridSpec(
            num_scalar_prefetch=4,
            grid=(n_work, ff // tf),
            in_specs=[pl.BlockSpec((sup, wp), lambda w, j, ws, we, l_, h_: (ws[w], 0)),
                      pl.BlockSpec((1, d, tf), lambda w, j, ws, we, l_, h_: (we[w], 0, j)),
                      pl.BlockSpec((1, d, tf), lambda w, j, ws, we, l_, h_: (we[w], 0, j)),
                      pl.BlockSpec((1, tf, d), lambda w, j, ws, we, l_, h_: (we[w], j, 0))],
            out_specs=pl.BlockSpec((sup, d), lambda w, j, ws, we, l_, h_: (ws[w], 0)),
            scratch_shapes=[pltpu.VMEM((d, tf), BF16), pltpu.VMEM((d, tf), BF16), pltpu.VMEM((tf, d), BF16)]),
        out_shape=jax.ShapeDtypeStruct((n_rows, d), F32),
        compiler_params=_params("arbitrary", "arbitrary"),
        name="moe_gemm",
    )(w_sup, w_exp, lo, hi, xs, w1, w3, w2)


def _moe_combine_kernel(pos_ref, x_ref, route_ref, mod_ref, g_ref, ys_ref, o_ref, buf_ref, sem, *, tile_off):
    i = pl.program_id(0)
    tm = x_ref.shape[0]
    base = (tile_off + i) * tm * TOP_K

    def copy(r, k):
        return pltpu.make_async_copy(ys_ref.at[pl.ds(pos_ref[base + r * TOP_K + k], 1)],
                                     buf_ref.at[k, pl.ds(r, 1)], sem)

    def start(r, c):
        for k in range(TOP_K):
            copy(r, k).start()
        return c

    def wait(r, c):
        for k in range(TOP_K):
            copy(r, k).wait()
        return c

    lax.fori_loop(0, tm, start, 0)
    lax.fori_loop(0, tm, wait, 0)
    route = route_ref[...]
    moe = route[:, 2:3] * buf_ref[0] + route[:, 3:4] * buf_ref[1]
    xn = x_ref[...] + mod_ref[0, 0][5:6] * moe
    y = xn * lax.rsqrt(jnp.mean(xn * xn, axis=-1, keepdims=True) + EPS)
    o_ref[...] = y * g_ref[...]


def _moe_combine(geom, x, route, mods, l, final_g, ys, pos_flat, row_off, n_out):
    d = x.shape[1]
    tm = ROW_TILE
    t0 = row_off // tm
    return pl.pallas_call(
        functools.partial(_moe_combine_kernel, tile_off=t0),
        grid_spec=pltpu.PrefetchScalarGridSpec(
            num_scalar_prefetch=1,
            grid=(n_out // tm,),
            in_specs=[pl.BlockSpec((tm, d), lambda i, pos: (t0 + i, 0)),
                      pl.BlockSpec((tm, 128), lambda i, pos: (t0 + i, 0)),
                      pl.BlockSpec((1, 1, 6, d), lambda i, pos: (l, geom.mod_row(t0 + i), 0, 0)),
                      pl.BlockSpec((1, d), lambda i, pos: (0, 0)),
                      pl.BlockSpec(memory_space=pl.ANY)],
            out_specs=pl.BlockSpec((tm, d), lambda i, pos: (i, 0)),
            scratch_shapes=[pltpu.VMEM((TOP_K, tm, d), F32), pltpu.SemaphoreType.DMA(())]),
        out_shape=jax.ShapeDtypeStruct((n_out, d), F32),
        compiler_params=_params("arbitrary"),
        name="moe_combine",
    )(pos_flat, x, route, mods, final_g.reshape(1, d), ys)


def _final_norm_kernel(x_ref, g_ref, o_ref):
    x = x_ref[...]
    o_ref[...] = (x * lax.rsqrt(jnp.mean(x * x, axis=-1, keepdims=True) + EPS)) * g_ref[...]


def _final_norm(x, final_g, row_off, n_out):
    d = x.shape[1]
    tm = ROW_TILE
    t0 = row_off // tm
    return pl.pallas_call(
        _final_norm_kernel,
        grid=(n_out // tm,),
        in_specs=[pl.BlockSpec((tm, d), lambda i: (t0 + i, 0)), pl.BlockSpec((1, d), lambda i: (0, 0))],
        out_specs=pl.BlockSpec((tm, d), lambda i: (i, 0)),
        out_shape=jax.ShapeDtypeStruct((n_out, d), F32),
        compiler_params=_params("arbitrary"),
        name="final_norm",
    )(x, final_g.reshape(1, d))


def _rope_tables(t):
    pairs = HEAD_DIM // 4
    pos = jnp.arange(t)
    inv = 1.0 / (ROPE_BASE ** (jnp.arange(pairs, dtype=F32) / pairs))
    ar = (pos // GRID_W).astype(F32)[:, None] * inv
    ac = (pos % GRID_W).astype(F32)[:, None] * inv
    cos = jnp.concatenate([jnp.cos(ar), jnp.cos(ar), jnp.cos(ac), jnp.cos(ac)], axis=1)
    sin = jnp.concatenate([-jnp.sin(ar), jnp.sin(ar), -jnp.sin(ac), jnp.sin(ac)], axis=1)
    reps = 128 // HEAD_DIM
    cos = jnp.concatenate([jnp.ones((ROW_TILE, 128), F32), jnp.tile(cos, (1, reps))], axis=0)
    sin = jnp.concatenate([jnp.zeros((ROW_TILE, 128), F32), jnp.tile(sin, (1, reps))], axis=0)
    return cos, sin


def _block_diag(w):
    n, b, _ = w.shape
    eye = jnp.eye(n, dtype=w.dtype)
    return (eye[:, None, :, None] * w[:, :, None, :]).reshape(n * b, n * b)


def kernel(x_prompt, x_sample, cache_k, cache_v, state_lru, c, c_ctx, norm1_g, norm2_g, w_mod, b_mod, w_in,
           conv_w, attn_sink, lru_conv_w, lru_conv_b, lru_wa, lru_ba, lru_wx, lru_bx, lru_lambda, w_out,
           ffn_w1, ffn_w3, ffn_w2, router_w, moe_w1, moe_w3, moe_w2, final_g):
    batch, seq, d = x_prompt.shape
    dec_batch, dec_seq, _ = x_sample.shape
    depth = w_in.shape[0]
    cw = conv_w.shape[2]
    lw = lru_lambda.shape[2]
    n_heads = attn_sink.shape[1]
    aw = n_heads * HEAD_DIM
    kw = N_KV_HEADS * HEAD_DIM
    past = cache_k.shape[2]
    n_experts = router_w.shape[2]
    assert cw == lw and dec_batch + 1 <= 8 and dec_seq // WINDOW_BLOCK >= 2
    geom = _Geom(batch, seq, dec_batch, dec_seq)

    cond = jnp.zeros((8, d), F32).at[0].set(c_ctx).at[1:1 + dec_batch].set(c)
    mods = _modulation(cond, w_mod, b_mod).reshape(depth, 8, 6, d)
    cos_t, sin_t = _rope_tables(dec_seq)
    cache_k = cache_k.reshape(dec_batch, depth, past, kw)
    cache_v = cache_v.reshape(dec_batch, depth, past, kw)
    zero_state = jnp.zeros((batch, 2, lw), F32)

    x = jnp.concatenate([x_prompt.reshape(geom.n_ctx, d), x_sample.reshape(geom.n_lat, d)], axis=0)
    new_k, new_v, new_lru = [], [], []
    y_prompt = y_sample = None
    for l in range(depth):
        sink_b = jnp.broadcast_to(attn_sink[l][:, None], (n_heads, 128))
        lru_p = (lru_conv_w[l], lru_conv_b[l].reshape(2, 1, lw),
                 jnp.concatenate([jnp.stack([_block_diag(lru_wa[l, dd]) for dd in range(2)]),
                                  jnp.stack([_block_diag(lru_wx[l, dd]) for dd in range(2)])], axis=2).astype(BF16),
                 jnp.concatenate([lru_ba[l], lru_bx[l]], axis=1).reshape(2, 1, 2 * lw),
                 lru_lambda[l].reshape(2, 1, lw))

        cbz, zb, q, k, v, lru_in = _stage_a(geom, x, mods, l, norm1_g[l], w_in[l].astype(BF16), cos_t, sin_t, cw, aw, kw)
        new_k.append(k[:geom.n_ctx].reshape(batch, seq, N_KV_HEADS, HEAD_DIM))
        new_v.append(v[:geom.n_ctx].reshape(batch, seq, N_KV_HEADS, HEAD_DIM))

        att_ctx = _attention_ctx(geom, q, k, v, sink_b)
        att_lat = _attention_lat(geom, q, k, v, cache_k, cache_v, l, sink_b)
        lru_ctx, h_ctx = _lru(lru_in, 0, batch, seq, zero_state, lru_p)
        lru_lat, _ = _lru(lru_in, geom.n_ctx, dec_batch, dec_seq, state_lru[:, l], lru_p)
        new_lru.append(h_ctx)
        y_att = jnp.concatenate([att_ctx, att_lat], axis=0)
        y_lru = jnp.concatenate([lru_ctx, lru_lat], axis=0)

        if l % 2 == 0:
            x, h2 = _stage_c(geom, x, cbz, zb, y_att, y_lru, mods, l, conv_w[l], w_out[l].astype(BF16), norm2_g[l],
                             None, n_experts)
            i = l // 2
            x = _ffn(geom, x, h2, mods, l, ffn_w1[i].astype(BF16), ffn_w3[i].astype(BF16), ffn_w2[i].astype(BF16))
            if l == depth - 1:
                y_prompt = _final_norm(x, final_g, 0, geom.n_ctx)
                y_sample = _final_norm(x, final_g, geom.n_ctx, geom.n_lat)
        else:
            i = l // 2
            router_bf = jnp.zeros((d, 128), F32).at[:, :n_experts].set(router_w[i]).astype(BF16)
            x, h2p, route = _stage_c(geom, x, cbz, zb, y_att, y_lru, mods, l, conv_w[l], w_out[l].astype(BF16),
                                     norm2_g[l], router_bf, n_experts)
            pos, n_rows, plan = _moe_plan(route, n_experts)
            pos_flat = pos.reshape(-1)
            xs = _moe_scatter(h2p, pos_flat, n_rows)
            ys = _moe_gemm(xs, plan, moe_w1[i], moe_w3[i], moe_w2[i])
            if l == depth - 1:
                y_prompt = _moe_combine(geom, x, route, mods, l, final_g, ys, pos_flat, 0, geom.n_ctx)
                y_sample = _moe_combine(geom, x, route, mods, l, final_g, ys, pos_flat, geom.n_ctx, geom.n_lat)
            else:
                raise NotImplementedError("a routed layer is only supported as the last layer")

    return (y_prompt.reshape(batch, seq, d), y_sample.reshape(dec_batch, dec_seq, d),
            jnp.stack(new_k, axis=1), jnp.stack(new_v, axis=1), jnp.stack(new_lru, axis=1))
```

```python
import functools

import jax
import jax.numpy as jnp
from jax import lax
from jax.experimental import pallas as pl
from jax.experimental.pallas import tpu as pltpu

F32 = jnp.float32
BF16 = jnp.bfloat16

HEAD_DIM = 64
N_KV_HEADS = 2
GQA_GROUP = 4
GRID_W = 64
ROPE_BASE = 10000.0
WINDOW_BLOCK = 128
LRU_C = 8.0
LRU_SEGMENTS = 8
EPS = 1e-6
NEG = -1e30
LOG2E = 1.4426950408889634
TOP_K = 2

ROW_TILE = 512
MOE_ROW_TILE = 512
MOE_SUPER = 2048
MOE_FF_TILE = 512
VMEM_LIMIT = 52 * 1024 * 1024
MOE_GEMM_VMEM_LIMIT = 58 * 1024 * 1024


def _params(*sem):
    return pltpu.CompilerParams(dimension_semantics=sem, vmem_limit_bytes=VMEM_LIMIT)


def _mod_kernel(cond_ref, w_ref, b_ref, o_ref):
    c = cond_ref[...]
    s = c * jax.nn.sigmoid(c)
    o_ref[0] = jnp.dot(s.astype(BF16), w_ref[0].astype(BF16), preferred_element_type=F32) + b_ref[0]


def _modulation(cond, w_mod, b_mod):
    depth, d, d6 = w_mod.shape
    tn = d6 // 4
    return pl.pallas_call(
        _mod_kernel,
        grid=(depth, d6 // tn),
        in_specs=[pl.BlockSpec((8, d), lambda l, j: (0, 0)),
                  pl.BlockSpec((1, d, tn), lambda l, j: (l, 0, j)),
                  pl.BlockSpec((1, 1, tn), lambda l, j: (l, 0, j))],
        out_specs=pl.BlockSpec((1, 8, tn), lambda l, j: (l, 0, j)),
        out_shape=jax.ShapeDtypeStruct((depth, 8, d6), F32),
        compiler_params=_params("arbitrary", "arbitrary"),
        name="modulation",
    )(cond, w_mod, b_mod.reshape(depth, 1, d6))


class _Geom:
    def __init__(self, n_ctx_seq, ctx_len, n_lat_seq, lat_len):
        self.n_ctx_seq, self.ctx_len, self.n_lat_seq, self.lat_len = n_ctx_seq, ctx_len, n_lat_seq, lat_len
        self.n_ctx = n_ctx_seq * ctx_len
        self.n_lat = n_lat_seq * lat_len
        self.n_tok = self.n_ctx + self.n_lat
        assert self.n_ctx % ROW_TILE == 0 and lat_len % ROW_TILE == 0 and ROW_TILE % ctx_len == 0
        assert self.n_ctx % lat_len == 0 or self.n_ctx < lat_len
        self.ctx_tiles = self.n_ctx // ROW_TILE
        self.lat_tiles_per_seq = lat_len // ROW_TILE
        self.n_tiles = self.n_tok // ROW_TILE

    def mod_row(self, i):
        return jnp.where(i < self.ctx_tiles, 0, 1 + (i - self.ctx_tiles) // self.lat_tiles_per_seq)

    def rope_block(self, i):
        return jnp.where(i < self.ctx_tiles, 0, 1 + (i - self.ctx_tiles) % self.lat_tiles_per_seq)


def _rms_mod(x, g, shift, scale):
    y = x * lax.rsqrt(jnp.mean(x * x, axis=-1, keepdims=True) + EPS)
    return (y * g) * (1 + scale) + shift


def _rope(x, cos, sin):
    lane = lax.broadcasted_iota(jnp.int32, x.shape, 1)
    swapped = jnp.where((lane & 31) < 16, pltpu.roll(x, 128 - 16, 1), pltpu.roll(x, 16, 1))
    return x * cos + swapped * sin


def _stream_specs(geom, x_parts, d):
    tm = ROW_TILE
    if len(x_parts) == 1:
        return [pl.BlockSpec((tm, d), lambda i, *_: (i, 0))]
    ct = geom.ctx_tiles
    return [pl.BlockSpec((tm, d), lambda i, *_: (jnp.minimum(i, ct - 1), 0)),
            pl.BlockSpec((tm, d), lambda i, *_: (jnp.maximum(i - ct, 0), 0))]


def _stream_tile(x_refs, ctx_tiles):
    if len(x_refs) == 1:
        return x_refs[0][...]
    return jnp.where(pl.program_id(0) < ctx_tiles, x_refs[0][...], x_refs[1][...])


def _stage_a_kernel(*refs, n_x, ctx_tiles, cw, aw, kw):
    x_refs = refs[:n_x]
    mod_ref, g_ref, w_ref, cos_ref, sin_ref, cbz_ref, zb_ref, q_ref, k_ref, v_ref, lru_ref = refs[n_x:]
    m = mod_ref[0, 0]
    h = _rms_mod(_stream_tile(x_refs, ctx_tiles), g_ref[...], m[0:1], m[1:2]).astype(BF16)
    rows = h.shape[0]

    def proj(lo, hi):
        return jnp.dot(h, w_ref[:, lo:hi], preferred_element_type=F32)

    cb = proj(0, cw)
    z = proj(cw, 2 * cw) * proj(2 * cw, 3 * cw)
    cbz_ref[:, 0:cw] = cb
    cbz_ref[:, cw:2 * cw] = z
    zb_ref[0, 0:8] = z[0:8]
    zb_ref[0, 8:16] = z[rows - 8:rows]

    cos, sin = cos_ref[...], sin_ref[...]
    o = 3 * cw
    for c in range(aw // 128):
        qc = _rope(proj(o + c * 128, o + (c + 1) * 128), cos, sin)
        q_ref[:, c * 128:(c + 1) * 128] = (qc * (HEAD_DIM ** -0.5 * LOG2E)).astype(BF16)
    o += aw
    k_ref[...] = _rope(proj(o, o + kw), cos, sin)
    v_ref[...] = proj(o + kw, o + 2 * kw)
    o += 2 * kw
    lru_ref[...] = proj(o, o + 2 * cw)


def _stage_a(geom, x_parts, mods, l, norm_g, w_in_bf, cos_t, sin_t, cw, aw, kw):
    n, d = geom.n_tok, x_parts[0].shape[1]
    tm = ROW_TILE
    d_in = w_in_bf.shape[1]
    row = lambda i: (i, 0)
    out_shape = (jax.ShapeDtypeStruct((n, 2 * cw), F32),
                 jax.ShapeDtypeStruct((geom.n_tiles, 16, cw), F32),
                 jax.ShapeDtypeStruct((n, aw), BF16),
                 jax.ShapeDtypeStruct((n, kw), F32),
                 jax.ShapeDtypeStruct((n, kw), F32),
                 jax.ShapeDtypeStruct((n, 2 * cw), F32))
    return pl.pallas_call(
        functools.partial(_stage_a_kernel, n_x=len(x_parts), ctx_tiles=geom.ctx_tiles, cw=cw, aw=aw, kw=kw),
        grid=(geom.n_tiles,),
        in_specs=_stream_specs(geom, x_parts, d) + [
                  pl.BlockSpec((1, 1, 6, d), lambda i: (l, geom.mod_row(i), 0, 0)),
                  pl.BlockSpec((1, d), lambda i: (0, 0)),
                  pl.BlockSpec((d, d_in), lambda i: (0, 0)),
                  pl.BlockSpec((tm, 128), lambda i: (geom.rope_block(i), 0)),
                  pl.BlockSpec((tm, 128), lambda i: (geom.rope_block(i), 0))],
        out_specs=(pl.BlockSpec((tm, 2 * cw), row),
                   pl.BlockSpec((1, 16, cw), lambda i: (i, 0, 0)),
                   pl.BlockSpec((tm, aw), row),
                   pl.BlockSpec((tm, kw), row),
                   pl.BlockSpec((tm, kw), row),
                   pl.BlockSpec((tm, 2 * cw), row)),
        out_shape=out_shape,
        compiler_params=_params("arbitrary"),
        name=f"stage_a_l{l}",
    )(*x_parts, mods, norm_g.reshape(1, d), w_in_bf, cos_t, sin_t)


def _qk(q, k):
    return lax.dot_general(q, k, (((1,), (1,)), ((), ())), preferred_element_type=F32)


def _attn_kernel(*refs, windowed, nb):
    n_in = 10 if windowed else 4
    ins, rest = refs[:n_in], refs[n_in:]
    if windowed:
        q_ref, kl_ref, km_ref, kr_ref, vl_ref, vm_ref, vr_ref, kc_ref, vc_ref, sink_ref = ins
    else:
        q_ref, km_ref, vm_ref, sink_ref = ins
    o_ref, ka_ref, kb_ref, va_ref, vb_ref = rest
    tq = q_ref.shape[0]

    def fill(r0, k, v):
        n = k.shape[0]
        lane = lax.broadcasted_iota(jnp.int32, (n, 128), 1)
        low = lane < HEAD_DIM
        k_sw, v_sw = pltpu.roll(k, HEAD_DIM, 1), pltpu.roll(v, HEAD_DIM, 1)
        one_hi = jnp.where(lane == HEAD_DIM, 1.0, 0.0)
        one_lo = jnp.where(lane == 0, 1.0, 0.0)
        rows = pl.ds(r0, n)
        for g, (kg_lo, kg_hi, vg_lo, vg_hi) in enumerate(((k, k_sw, v, v_sw), (k_sw, k, v_sw, v))):
            ka_ref[g, rows, :] = jnp.where(low, kg_lo, 0.0).astype(BF16)
            kb_ref[g, rows, :] = jnp.where(low, 0.0, kg_hi).astype(BF16)
            va_ref[g, rows, :] = jnp.where(low, vg_lo, one_hi).astype(BF16)
            vb_ref[g, rows, :] = jnp.where(low, one_lo, vg_hi).astype(BF16)

    if windowed:
        j = pl.program_id(1)
        fill(0, kl_ref[...], vl_ref[...])
        fill(tq, km_ref[...], vm_ref[...])
        fill(2 * tq, kr_ref[...], vr_ref[...])

        @pl.when(j == 0)
        def _():
            fill(3 * tq, kc_ref[0, 0], vc_ref[0, 0])

        r = lax.broadcasted_iota(jnp.int32, (2 * tq, tq), 0) & (tq - 1)
        c = lax.broadcasted_iota(jnp.int32, (2 * tq, tq), 1)
        left_ok = jnp.logical_and(c >= r, j > 0)
        right_ok = jnp.logical_and(c <= r, j < nb - 1)
    else:
        fill(0, km_ref[...], vm_ref[...])

    sink_all = sink_ref[...] * LOG2E
    lane = lax.broadcasted_iota(jnp.int32, (2 * tq, 128), 1)
    for g in range(N_KV_HEADS):
        q2 = jnp.concatenate([q_ref[:, (2 * g) * 128:(2 * g + 1) * 128],
                              q_ref[:, (2 * g + 1) * 128:(2 * g + 2) * 128]], axis=0)
        res = []
        for odd, (k_ref, v_ref, den_lane) in enumerate(((ka_ref, va_ref, HEAD_DIM), (kb_ref, vb_ref, 0))):
            s = _qk(q2, k_ref[g])
            if windowed:
                s = jnp.concatenate([jnp.where(left_ok, s[:, 0:tq], NEG), s[:, tq:2 * tq],
                                     jnp.where(right_ok, s[:, 2 * tq:3 * tq], NEG), s[:, 3 * tq:]], axis=1)
            h0 = GQA_GROUP * g + odd
            sink_col = jnp.concatenate([jnp.broadcast_to(sink_all[h0:h0 + 1, 0:1], (tq, 1)),
                                        jnp.broadcast_to(sink_all[h0 + 2:h0 + 3, 0:1], (tq, 1))], axis=0)
            m = jnp.maximum(jnp.max(s, axis=-1, keepdims=True), sink_col)
            o = jnp.dot(jnp.exp2(s - m).astype(BF16), v_ref[g], preferred_element_type=F32)
            res.append(o / (o[:, den_lane:den_lane + 1] + jnp.exp2(sink_col - m)))
        merged = jnp.where(lane < HEAD_DIM, res[0], res[1]).astype(BF16)
        o_ref[:, (2 * g) * 128:(2 * g + 1) * 128] = merged[0:tq]
        o_ref[:, (2 * g + 1) * 128:(2 * g + 2) * 128] = merged[tq:2 * tq]


def _attn_scratch(n_keys):
    return [pltpu.VMEM((N_KV_HEADS, n_keys, 128), BF16)] * 4


def _attention_ctx(geom, q, k, v, sink_b):
    t, aw, kw = geom.ctx_len, q.shape[1], k.shape[1]
    assert kw == 128 and aw == 2 * N_KV_HEADS * 128
    blk = lambda b: (b, 0)
    return pl.pallas_call(
        functools.partial(_attn_kernel, windowed=False, nb=1),
        grid=(geom.n_ctx_seq,),
        in_specs=[pl.BlockSpec((t, aw), blk), pl.BlockSpec((t, kw), blk), pl.BlockSpec((t, kw), blk),
                  pl.BlockSpec(sink_b.shape, lambda b: (0, 0))],
        out_specs=pl.BlockSpec((t, aw), blk),
        out_shape=jax.ShapeDtypeStruct((geom.n_ctx, aw), BF16),
        scratch_shapes=_attn_scratch(t),
        compiler_params=_params("arbitrary"),
        name="attention_ctx",
    )(q, k, v, sink_b)


def _attention_lat(geom, q, k, v, cache_k, cache_v, l, sink_b):
    aw, kw = q.shape[1], k.shape[1]
    tq = WINDOW_BLOCK
    nb = geom.lat_len // tq
    off = geom.n_ctx // tq
    past = cache_k.shape[2]

    def at(delta):
        return lambda b, j: (off + b * nb + jnp.clip(j + delta, 0, nb - 1), 0)

    kv_spec = [pl.BlockSpec((tq, kw), at(-1)), pl.BlockSpec((tq, kw), at(0)), pl.BlockSpec((tq, kw), at(1))]
    cache_spec = pl.BlockSpec((1, 1, past, kw), lambda b, j: (b, l, 0, 0))
    return pl.pallas_call(
        functools.partial(_attn_kernel, windowed=True, nb=nb),
        grid=(geom.n_lat_seq, nb),
        in_specs=[pl.BlockSpec((tq, aw), at(0))] + kv_spec + kv_spec + [cache_spec, cache_spec,
                  pl.BlockSpec(sink_b.shape, lambda b, j: (0, 0))],
        out_specs=pl.BlockSpec((tq, aw), lambda b, j: (b * nb + j, 0)),
        out_shape=jax.ShapeDtypeStruct((geom.n_lat, aw), BF16),
        scratch_shapes=_attn_scratch(3 * tq + past),
        compiler_params=_params("arbitrary", "arbitrary"),
        name="attention_lat",
    )(q, k, k, k, v, v, v, cache_k, cache_v, sink_b)


def _log_sigmoid(x):
    return jnp.minimum(x, 0.0) - jnp.log(1.0 + jnp.exp(-jnp.abs(x)))


def _lru_kernel(x_ref, h0_ref, cw_ref, cb_ref, wg_ref, bg_ref, lam_ref, y_ref, ht_ref,
                xp_ref, af_ref, uf_ref, ab_ref, ub_ref, hin_ref, *, t, lw):
    seg = LRU_SEGMENTS
    seg_len = t // seg
    tg = min(t, 256)
    tc = min(seg_len, 256)
    nh = lw // 128

    def put(ref, r0, rows, val):
        for hh in range(nh):
            ref[hh, pl.ds(r0, rows), :] = val[:, hh * 128:(hh + 1) * 128]

    def get(ref, r0, rows):
        return jnp.concatenate([ref[hh, pl.ds(r0, rows), :] for hh in range(nh)], axis=1)

    zeros8 = jnp.zeros((8, lw), F32)
    xp_ref[0:8] = zeros8
    xp_ref[t + 8:t + 16] = zeros8
    xp_ref[8:t + 8] = x_ref[:, 0:lw]

    dirs = ((af_ref, uf_ref), (ab_ref, ub_ref))
    log_sig = [_log_sigmoid(lam_ref[d]) for d in range(2)]
    taps = cw_ref.shape[1]

    def gate_chunk(c, carry):
        r0 = pl.multiple_of(c * tg, tg)
        win = xp_ref[pl.ds(r0, tg + 16), :]
        for d, (a_ref, u_ref) in enumerate(dirs):
            cw = cw_ref[d]
            xc = None
            for k in range(taps):
                s0 = 8 - (taps - 1) + k if d == 0 else 8 + k
                term = win[s0:s0 + tg] * cw[k:k + 1]
                xc = term if xc is None else xc + term
            xc = xc + cb_ref[d]
            gates = jnp.dot(xc.astype(BF16), wg_ref[d], preferred_element_type=F32) + bg_ref[d]
            r = 0.5 * jnp.tanh(0.5 * gates[:, 0:lw]) + 0.5
            i = 0.5 * jnp.tanh(0.5 * gates[:, lw:2 * lw]) + 0.5
            log_a = (LRU_C * r) * log_sig[d]
            a = jnp.exp(log_a)
            u = jnp.sqrt(-jnp.tanh(log_a) * (a * a + 1.0)) * (i * xc)
            put(a_ref, r0, tg, a)
            put(u_ref, r0, tg, u)
        return carry

    lax.fori_loop(0, t // tg, gate_chunk, 0)

    def scan_step(jj, carry):
        out = []
        for d, (a_ref, u_ref) in enumerate(dirs):
            j = jj if d == 0 else seg_len - 1 - jj
            idx = pl.ds(j, seg, stride=seg_len)
            for hh in range(nh):
                p, u = carry[len(out)], carry[len(out) + 1]
                a = a_ref[hh, idx, :]
                p = a * p
                u = a * u + u_ref[hh, idx, :]
                a_ref[hh, idx, :] = p
                u_ref[hh, idx, :] = u
                out += [p, u]
        return tuple(out)

    init = (jnp.ones((seg, 128), F32), jnp.zeros((seg, 128), F32)) * (2 * nh)
    ends = lax.fori_loop(0, seg_len, scan_step, init, unroll=4)

    for d in range(2):
        p_end = jnp.concatenate(ends[2 * nh * d:2 * nh * (d + 1):2], axis=1)
        u_end = jnp.concatenate(ends[2 * nh * d + 1:2 * nh * (d + 1):2], axis=1)
        hcur = h0_ref[0, d:d + 1, :]
        for s in (range(seg) if d == 0 else range(seg - 1, -1, -1)):
            hin_ref[d, s:s + 1, :] = hcur
            hcur = p_end[s:s + 1] * hcur + u_end[s:s + 1]
        ht_ref[0, d:d + 1, :] = hcur

    def out_chunk(c, carry):
        r0 = pl.multiple_of(c * tc, tc)
        s = r0 // seg_len
        hf = get(af_ref, r0, tc) * hin_ref[0, pl.ds(s, 1), :] + get(uf_ref, r0, tc)
        hb = get(ab_ref, r0, tc) * hin_ref[1, pl.ds(s, 1), :] + get(ub_ref, r0, tc)
        lg = x_ref[pl.ds(r0, tc), lw:2 * lw]
        y_ref[pl.ds(r0, tc), :] = ((hf + hb) * jax.nn.gelu(lg)).astype(BF16)
        return carry

    lax.fori_loop(0, t // tc, out_chunk, 0)


def _lru(lru_in, row_off, n_seq, t, h0, lru_p):
    conv_w, conv_b, w_gate, b_gate, lam = lru_p
    lw = lru_in.shape[1] // 2
    assert row_off % t == 0 and t % (8 * LRU_SEGMENTS) == 0
    off = row_off // t
    whole = lambda a: pl.BlockSpec(a.shape, lambda b: (0,) * a.ndim)
    return pl.pallas_call(
        functools.partial(_lru_kernel, t=t, lw=lw),
        grid=(n_seq,),
        in_specs=[pl.BlockSpec((t, 2 * lw), lambda b: (off + b, 0)),
                  pl.BlockSpec((1, 2, lw), lambda b: (b, 0, 0)),
                  whole(conv_w), whole(conv_b), whole(w_gate), whole(b_gate), whole(lam)],
        out_specs=(pl.BlockSpec((t, lw), lambda b: (b, 0)),
                   pl.BlockSpec((1, 2, lw), lambda b: (b, 0, 0))),
        out_shape=(jax.ShapeDtypeStruct((n_seq * t, lw), BF16),
                   jax.ShapeDtypeStruct((n_seq, 2, lw), F32)),
        scratch_shapes=[pltpu.VMEM((t + 16, lw), F32)] + [pltpu.VMEM((lw // 128, t, 128), F32)] * 4
                       + [pltpu.VMEM((2, LRU_SEGMENTS, lw), F32)],
        compiler_params=_params("arbitrary"),
        name=f"rglru_t{t}",
    )(lru_in, h0, conv_w, conv_b, w_gate, b_gate, lam)


def _route_top2(h2b, router_ref, n_experts):
    logits = jnp.dot(h2b, router_ref[...], preferred_element_type=F32)
    lane = lax.broadcasted_iota(jnp.int32, logits.shape, 1).astype(F32)
    lg = jnp.where(lane < n_experts, logits, -jnp.inf)
    v1 = jnp.max(lg, axis=-1, keepdims=True)
    e1 = jnp.min(jnp.where(lg == v1, lane, 1e9), axis=-1, keepdims=True)
    lg2 = jnp.where(lane == e1, -jnp.inf, lg)
    v2 = jnp.max(lg2, axis=-1, keepdims=True)
    e2 = jnp.min(jnp.where(lg2 == v2, lane, 1e9), axis=-1, keepdims=True)
    ex = jnp.exp(v2 - v1)
    den = 1.0 + ex
    g1 = 1.0 / den
    g2 = ex / den
    return jnp.where(lane == 0, e1, jnp.where(lane == 1, e2, jnp.where(lane == 2, g1, jnp.where(lane == 3, g2, 0.0))))


def _stage_c_kernel(*refs, n_x, geom, cw, aw, routed, n_experts):
    x_refs, att_refs, lru_refs, refs = refs[:n_x], refs[n_x:n_x + 2], refs[n_x + 2:n_x + 4], refs[n_x + 4:]
    if routed:
        (cbz_ref, zprev_ref, znext_ref, mod_ref, convw_ref, wo_ref, g_ref, router_ref,
         xo_ref, h2_ref, route_ref) = refs
    else:
        (cbz_ref, zprev_ref, znext_ref, mod_ref, convw_ref, wo_ref, g_ref, xo_ref, h2_ref) = refs
    i = pl.program_id(0)
    tm = cbz_ref.shape[0]
    cb = cbz_ref[:, 0:cw]
    z = cbz_ref[:, cw:2 * cw]
    row = lax.broadcasted_iota(jnp.int32, (tm, 1), 0)
    seq_len = jnp.where(i < geom.ctx_tiles, geom.ctx_len, geom.lat_len)
    tpos = (i * tm + row) & (seq_len - 1)
    z_dn = jnp.where(row == 0, zprev_ref[0, 15:16], pltpu.roll(z, 1, 0))
    z_up = jnp.where(row == tm - 1, znext_ref[0, 0:1], pltpu.roll(z, tm - 1, 0))
    z_dn = jnp.where(tpos == 0, 0.0, z_dn)
    z_up = jnp.where(tpos == seq_len - 1, 0.0, z_up)
    cwt = convw_ref[...]
    y_conv = cb * (z_dn * cwt[0:1] + z * cwt[1:2] + z_up * cwt[2:3])

    mix = (jnp.dot(y_conv.astype(BF16), wo_ref[0:cw], preferred_element_type=F32)
           + jnp.dot(_stream_tile(att_refs, geom.ctx_tiles), wo_ref[cw:cw + aw], preferred_element_type=F32)
           + jnp.dot(_stream_tile(lru_refs, geom.ctx_tiles), wo_ref[cw + aw:2 * cw + aw],
                     preferred_element_type=F32))
    m = mod_ref[0, 0]
    xn = _stream_tile(x_refs, geom.ctx_tiles) + m[2:3] * mix
    xo_ref[...] = xn
    h2 = _rms_mod(xn, g_ref[...], m[3:4], m[4:5])
    if routed:
        h2_ref[...] = h2
        route_ref[...] = _route_top2(h2.astype(BF16), router_ref, n_experts)
    else:
        h2_ref[...] = h2.astype(BF16)


def _stage_c(geom, x_parts, cbz, zb, y_att, y_lru, mods, l, conv_w, w_out_bf, norm_g, router_bf, n_experts):
    n, d = geom.n_tok, x_parts[0].shape[1]
    tm = ROW_TILE
    cw, aw = conv_w.shape[1], y_att[0].shape[1]
    routed = router_bf is not None
    row = lambda i: (i, 0)
    last = geom.n_tiles - 1
    in_specs = _stream_specs(geom, x_parts, d) + _stream_specs(geom, y_att, aw) + _stream_specs(geom, y_lru, cw) + [
                pl.BlockSpec((tm, 2 * cw), row),
                pl.BlockSpec((1, 16, cw), lambda i: (jnp.maximum(i - 1, 0), 0, 0)),
                pl.BlockSpec((1, 16, cw), lambda i: (jnp.minimum(i + 1, last), 0, 0)),
                pl.BlockSpec((1, 1, 6, d), lambda i: (l, geom.mod_row(i), 0, 0)),
                pl.BlockSpec(conv_w.shape, lambda i: (0, 0)),
                pl.BlockSpec(w_out_bf.shape, lambda i: (0, 0)),
                pl.BlockSpec((1, d), lambda i: (0, 0))]
    args = [*x_parts, *y_att, *y_lru, cbz, zb, zb, mods, conv_w, w_out_bf, norm_g.reshape(1, d)]
    out_specs = [pl.BlockSpec((tm, d), row)]
    out_shape = [jax.ShapeDtypeStruct((n, d), F32)]
    if routed:
        in_specs.append(pl.BlockSpec(router_bf.shape, lambda i: (0, 0)))
        args.append(router_bf)
        out_specs += [pl.BlockSpec((tm, d), row), pl.BlockSpec((tm, 128), row)]
        out_shape += [jax.ShapeDtypeStruct((n, d), F32), jax.ShapeDtypeStruct((n, 128), F32)]
    else:
        out_specs.append(pl.BlockSpec((tm, d), row))
        out_shape.append(jax.ShapeDtypeStruct((n, d), BF16))
    return pl.pallas_call(
        functools.partial(_stage_c_kernel, n_x=len(x_parts), geom=geom, cw=cw, aw=aw, routed=routed,
                          n_experts=n_experts),
        grid=(geom.n_tiles,),
        in_specs=in_specs, out_specs=tuple(out_specs), out_shape=tuple(out_shape),
        compiler_params=_params("arbitrary"),
        name=f"stage_c_l{l}",
    )(*args)


def _silu(x):
    return x * jax.nn.sigmoid(x)


def _ffn_kernel(x_ref, h_ref, mod_ref, w1_ref, w3_ref, w2_ref, o_ref, *, fc):
    h = h_ref[...]
    ff = w1_ref.shape[1]
    acc = None
    for c in range(ff // fc):
        sl = slice(c * fc, (c + 1) * fc)
        a = _silu(jnp.dot(h, w1_ref[:, sl], preferred_element_type=F32)) * jnp.dot(
            h, w3_ref[:, sl], preferred_element_type=F32)
        part = jnp.dot(a.astype(BF16), w2_ref[sl, :], preferred_element_type=F32)
        acc = part if acc is None else acc + part
    o_ref[...] = x_ref[...] + mod_ref[0, 0][5:6] * acc


def _ffn(geom, x, h2, mods, l, w1, w3, w2):
    n, d = x.shape
    tm = ROW_TILE
    ff = w1.shape[1]
    fc = 256
    assert ff % fc == 0
    row = lambda i: (i, 0)
    const = lambda a: pl.BlockSpec(a.shape, lambda i: (0, 0), pipeline_mode=pl.Buffered(1))
    return pl.pallas_call(
        functools.partial(_ffn_kernel, fc=fc),
        grid=(geom.n_tiles,),
        in_specs=[pl.BlockSpec((tm, d), row), pl.BlockSpec((tm, d), row),
                  pl.BlockSpec((1, 1, 6, d), lambda i: (l, geom.mod_row(i), 0, 0)),
                  const(w1), const(w3), const(w2)],
        out_specs=pl.BlockSpec((tm, d), row),
        out_shape=jax.ShapeDtypeStruct((n, d), F32),
        compiler_params=_params("arbitrary"),
        name=f"ffn_l{l}",
    )(x, h2, mods, w1, w3, w2)


def _moe_plan(route, n_experts):
    n = route.shape[0]
    tmg, sup = MOE_ROW_TILE, MOE_SUPER
    i32 = jnp.int32
    experts = route[:, 0:TOP_K].astype(i32)
    onehot = (experts[:, :, None] == jnp.arange(n_experts, dtype=i32)[None, None, :]).astype(i32)
    sel = jnp.sum(onehot, axis=1)
    incl = jnp.cumsum(sel, axis=0)
    rank = incl - sel
    counts = incl[-1]
    tiles = (counts + tmg - 1) // tmg
    tile_end = jnp.cumsum(tiles)
    tile_off = tile_end - tiles
    pos = jnp.sum(onehot * ((tile_off * tmg)[None, :] + rank)[:, None, :], axis=2)

    n_rows = (TOP_K * n + n_experts * (tmg - 1)) // tmg * tmg
    n_rows = (n_rows + sup - 1) // sup * sup
    n_super = n_rows // sup
    per = sup // tmg
    n_work = n_super + n_experts - 1
    used = tile_end[-1]
    last_sup = (used - 1) // per
    end_ext = jnp.where((tile_end == used) & (tiles > 0), (last_sup + 1) * per, tile_end)
    s0 = (jnp.arange(n_super, dtype=i32) * per)[:, None]
    lo = jnp.clip(tile_off[None, :] - s0, 0, per).reshape(-1)
    hi = jnp.clip(end_ext[None, :] - s0, 0, per).reshape(-1)
    live = ((hi > lo) & jnp.tile(tiles > 0, n_super)).astype(i32)
    slot = jnp.cumsum(live) - live
    n_live = jnp.sum(live)
    w = jnp.arange(n_work, dtype=i32)
    hit = (slot[None, :] == jnp.minimum(w, n_live - 1)[:, None]) & (live[None, :] > 0)

    def pick(v):
        return jnp.sum(jnp.where(hit, v[None, :], 0), axis=1).astype(i32)

    pair = jnp.arange(n_super * n_experts, dtype=i32)
    busy = w < n_live
    return pos.astype(i32), n_rows, (pick(pair // n_experts), pick(pair % n_experts),
                                     jnp.where(busy, pick(lo), 0), jnp.where(busy, pick(hi), 0))


def _moe_scatter_kernel(pos_ref, h_ref, xs_in_ref, xs_ref, sem, *, tile_off):
    del xs_in_ref
    i = pl.program_id(0)
    tm = h_ref.shape[0]
    base = (tile_off + i) * tm * TOP_K

    def copy(r, k):
        return pltpu.make_async_copy(h_ref.at[pl.ds(r, 1)], xs_ref.at[pl.ds(pos_ref[base + r * TOP_K + k], 1)], sem)

    def start(r, c):
        for k in range(TOP_K):
            copy(r, k).start(priority=k)
        return c

    def wait(r, c):
        for k in range(TOP_K):
            copy(r, k).wait()
        return c

    lax.fori_loop(0, tm, start, 0, unroll=8)
    lax.fori_loop(0, tm, wait, 0, unroll=8)


def _moe_scatter(h2p, pos_flat, n_rows):
    n, w = h2p.shape
    tm = ROW_TILE
    xs0 = jnp.zeros((n_rows, w), F32)
    return pl.pallas_call(
        functools.partial(_moe_scatter_kernel, tile_off=0),
        grid_spec=pltpu.PrefetchScalarGridSpec(
            num_scalar_prefetch=1,
            grid=(n // tm,),
            in_specs=[pl.BlockSpec((tm, w), lambda i, pos: (i, 0)),
                      pl.BlockSpec(memory_space=pl.ANY)],
            out_specs=pl.BlockSpec(memory_space=pl.ANY),
            scratch_shapes=[pltpu.SemaphoreType.DMA(())]),
        out_shape=jax.ShapeDtypeStruct((n_rows, w), F32),
        input_output_aliases={2: 0},
        compiler_params=_params("arbitrary"),
        name="moe_scatter",
    )(pos_flat, h2p, xs0)


def _moe_gemm_kernel(wsup_ref, wexp_ref, lo_ref, hi_ref, xs_ref, w1_ref, w3_ref, w2_ref, o_ref,
                     w1b_ref, w3b_ref, w2b_ref):
    w = pl.program_id(0)
    j = pl.program_id(1)
    tmg = MOE_ROW_TILE
    lo, hi = lo_ref[w], hi_ref[w]
    fc = 256

    @pl.when(hi > lo)
    def _():
        w1b_ref[...] = w1_ref[0].astype(BF16)
        w3b_ref[...] = w3_ref[0].astype(BF16)
        w2b_ref[...] = w2_ref[0].astype(BF16)

        def sub(s, carry):
            r0 = pl.multiple_of(s * tmg, tmg)
            xb = xs_ref[pl.ds(r0, tmg), :].astype(BF16)
            part = None
            for c in range(w1b_ref.shape[1] // fc):
                sl = slice(c * fc, (c + 1) * fc)
                h1 = jnp.dot(xb, w1b_ref[:, sl], preferred_element_type=F32)
                h3 = jnp.dot(xb, w3b_ref[:, sl], preferred_element_type=F32)
                pc = jnp.dot((_silu(h1) * h3).astype(BF16), w2b_ref[sl, :], preferred_element_type=F32)
                part = pc if part is None else part + pc

            @pl.when(j == 0)
            def _():
                o_ref[pl.ds(r0, tmg), :] = part

            @pl.when(j > 0)
            def _():
                o_ref[pl.ds(r0, tmg), :] += part

            return carry

        lax.fori_loop(lo, hi, sub, 0)


def _moe_gemm(xs, plan, w1, w3, w2):
    n_rows, wp = xs.shape
    n_exp, d, ff = w1.shape
    sup, tf = MOE_SUPER, MOE_FF_TILE
    w_sup, w_exp, lo, hi = plan
    n_work = w_sup.shape[0]
    return pl.pallas_call(
        _moe_gemm_kernel,
        grid_spec=pltpu.PrefetchScalarGridSpec(
            num_scalar_prefetch=4,
            grid=(n_work, ff // tf),
            in_specs=[pl.BlockSpec((sup, wp), lambda w, j, ws, we, l_, h_: (ws[w], 0)),
                      pl.BlockSpec((1, d, tf), lambda w, j, ws, we, l_, h_: (we[w], 0, j)),
                      pl.BlockSpec((1, d, tf), lambda w, j, ws, we, l_, h_: (we[w], 0, j)),
                      pl.BlockSpec((1, tf, d), lambda w, j, ws, we, l_, h_: (we[w], j, 0))],
            out_specs=pl.BlockSpec((sup, d), lambda w, j, ws, we, l_, h_: (ws[w], 0)),
            scratch_shapes=[pltpu.VMEM((d, tf), BF16), pltpu.VMEM((d, tf), BF16), pltpu.VMEM((tf, d), BF16)]),
        out_shape=jax.ShapeDtypeStruct((n_rows, d), F32),
        compiler_params=pltpu.CompilerParams(dimension_semantics=("arbitrary", "arbitrary"),
                                             vmem_limit_bytes=MOE_GEMM_VMEM_LIMIT),
        name="moe_gemm",
    )(w_sup, w_exp, lo, hi, xs, w1, w3, w2)


def _moe_combine_kernel(pos_ref, x_ref, route_ref, mod_ref, g_ref, ys_ref, o_ref, buf_ref, sem,
                        *, tile_off, n_tiles):
    i = pl.program_id(0)
    tm = x_ref.shape[0]

    def copy(tile, slot, r, k):
        base = (tile_off + tile) * tm * TOP_K
        return pltpu.make_async_copy(ys_ref.at[pl.ds(pos_ref[base + r * TOP_K + k], 1)],
                                     buf_ref.at[slot, k, pl.ds(r, 1)], sem.at[slot])

    def issue(tile, slot):
        def start(r, c):
            for k in range(TOP_K):
                copy(tile, slot, r, k).start(priority=k)
            return c

        lax.fori_loop(0, tm, start, 0, unroll=8)

    @pl.when(i == 0)
    def _():
        issue(0, 0)

    @pl.when(i + 1 < n_tiles)
    def _():
        issue(i + 1, (i + 1) % 2)

    slot = i % 2

    def wait(r, c):
        for k in range(TOP_K):
            copy(i, slot, r, k).wait()
        return c

    lax.fori_loop(0, tm, wait, 0, unroll=8)
    route = route_ref[...]
    moe = route[:, 2:3] * buf_ref[slot, 0] + route[:, 3:4] * buf_ref[slot, 1]
    xn = x_ref[...] + mod_ref[0, 0][5:6] * moe
    y = xn * lax.rsqrt(jnp.mean(xn * xn, axis=-1, keepdims=True) + EPS)
    o_ref[...] = y * g_ref[...]


def _moe_combine(geom, x, route, mods, l, final_g, ys, pos_flat, row_off, n_out):
    d = x.shape[1]
    tm = ROW_TILE
    t0 = row_off // tm
    return pl.pallas_call(
        functools.partial(_moe_combine_kernel, tile_off=t0, n_tiles=n_out // tm),
        grid_spec=pltpu.PrefetchScalarGridSpec(
            num_scalar_prefetch=1,
            grid=(n_out // tm,),
            in_specs=[pl.BlockSpec((tm, d), lambda i, pos: (t0 + i, 0)),
                      pl.BlockSpec((tm, 128), lambda i, pos: (t0 + i, 0)),
                      pl.BlockSpec((1, 1, 6, d), lambda i, pos: (l, geom.mod_row(t0 + i), 0, 0)),
                      pl.BlockSpec((1, d), lambda i, pos: (0, 0)),
                      pl.BlockSpec(memory_space=pl.ANY)],
            out_specs=pl.BlockSpec((tm, d), lambda i, pos: (i, 0)),
            scratch_shapes=[pltpu.VMEM((2, TOP_K, tm, d), F32), pltpu.SemaphoreType.DMA((2,))]),
        out_shape=jax.ShapeDtypeStruct((n_out, d), F32),
        compiler_params=_params("arbitrary"),
        name="moe_combine",
    )(pos_flat, x, route, mods, final_g.reshape(1, d), ys)


def _final_norm_kernel(x_ref, g_ref, o_ref):
    x = x_ref[...]
    o_ref[...] = (x * lax.rsqrt(jnp.mean(x * x, axis=-1, keepdims=True) + EPS)) * g_ref[...]


def _final_norm(x, final_g, row_off, n_out):
    d = x.shape[1]
    tm = ROW_TILE
    t0 = row_off // tm
    return pl.pallas_call(
        _final_norm_kernel,
        grid=(n_out // tm,),
        in_specs=[pl.BlockSpec((tm, d), lambda i: (t0 + i, 0)), pl.BlockSpec((1, d), lambda i: (0, 0))],
        out_specs=pl.BlockSpec((tm, d), lambda i: (i, 0)),
        out_shape=jax.ShapeDtypeStruct((n_out, d), F32),
        compiler_params=_params("arbitrary"),
        name="final_norm",
    )(x, final_g.reshape(1, d))


def _rope_tables(t):
    pairs = HEAD_DIM // 4
    pos = jnp.arange(t)
    inv = 1.0 / (ROPE_BASE ** (jnp.arange(pairs, dtype=F32) / pairs))
    ar = (pos // GRID_W).astype(F32)[:, None] * inv
    ac = (pos % GRID_W).astype(F32)[:, None] * inv
    cos = jnp.concatenate([jnp.cos(ar), jnp.cos(ar), jnp.cos(ac), jnp.cos(ac)], axis=1)
    sin = jnp.concatenate([-jnp.sin(ar), jnp.sin(ar), -jnp.sin(ac), jnp.sin(ac)], axis=1)
    reps = 128 // HEAD_DIM
    cos = jnp.concatenate([jnp.ones((ROW_TILE, 128), F32), jnp.tile(cos, (1, reps))], axis=0)
    sin = jnp.concatenate([jnp.zeros((ROW_TILE, 128), F32), jnp.tile(sin, (1, reps))], axis=0)
    return cos, sin


def _block_diag(w):
    n, b, _ = w.shape
    eye = jnp.eye(n, dtype=w.dtype)
    return (eye[:, None, :, None] * w[:, :, None, :]).reshape(n * b, n * b)


def kernel(x_prompt, x_sample, cache_k, cache_v, state_lru, c, c_ctx, norm1_g, norm2_g, w_mod, b_mod, w_in,
           conv_w, attn_sink, lru_conv_w, lru_conv_b, lru_wa, lru_ba, lru_wx, lru_bx, lru_lambda, w_out,
           ffn_w1, ffn_w3, ffn_w2, router_w, moe_w1, moe_w3, moe_w2, final_g):
    batch, seq, d = x_prompt.shape
    dec_batch, dec_seq, _ = x_sample.shape
    depth = w_in.shape[0]
    cw = conv_w.shape[2]
    lw = lru_lambda.shape[2]
    n_heads = attn_sink.shape[1]
    aw = n_heads * HEAD_DIM
    kw = N_KV_HEADS * HEAD_DIM
    past = cache_k.shape[2]
    n_experts = router_w.shape[2]
    assert cw == lw and dec_batch + 1 <= 8 and dec_seq // WINDOW_BLOCK >= 2
    geom = _Geom(batch, seq, dec_batch, dec_seq)

    cond = jnp.zeros((8, d), F32).at[0].set(c_ctx).at[1:1 + dec_batch].set(c)
    mods = _modulation(cond, w_mod, b_mod).reshape(depth, 8, 6, d)
    cos_t, sin_t = _rope_tables(dec_seq)
    cache_k = cache_k.reshape(dec_batch, depth, past, kw)
    cache_v = cache_v.reshape(dec_batch, depth, past, kw)
    zero_state = jnp.zeros((batch, 2, lw), F32)

    x = (x_prompt.reshape(geom.n_ctx, d), x_sample.reshape(geom.n_lat, d))
    new_k, new_v, new_lru = [], [], []
    y_prompt = y_sample = None
    for l in range(depth):
        sink_b = jnp.broadcast_to(attn_sink[l][:, None], (n_heads, 128))
        lru_p = (lru_conv_w[l], lru_conv_b[l].reshape(2, 1, lw),
                 jnp.concatenate([jnp.stack([_block_diag(lru_wa[l, dd]) for dd in range(2)]),
                                  jnp.stack([_block_diag(lru_wx[l, dd]) for dd in range(2)])], axis=2).astype(BF16),
                 jnp.concatenate([lru_ba[l], lru_bx[l]], axis=1).reshape(2, 1, 2 * lw),
                 lru_lambda[l].reshape(2, 1, lw))

        cbz, zb, q, k, v, lru_in = _stage_a(geom, x, mods, l, norm1_g[l], w_in[l].astype(BF16), cos_t, sin_t, cw, aw, kw)
        new_k.append(k[:geom.n_ctx].reshape(batch, seq, N_KV_HEADS, HEAD_DIM))
        new_v.append(v[:geom.n_ctx].reshape(batch, seq, N_KV_HEADS, HEAD_DIM))

        y_att = (_attention_ctx(geom, q, k, v, sink_b), _attention_lat(geom, q, k, v, cache_k, cache_v, l, sink_b))
        lru_ctx, h_ctx = _lru(lru_in, 0, batch, seq, zero_state, lru_p)
        lru_lat, _ = _lru(lru_in, geom.n_ctx, dec_batch, dec_seq, state_lru[:, l], lru_p)
        y_lru = (lru_ctx, lru_lat)
        new_lru.append(h_ctx)

        if l % 2 == 0:
            x, h2 = _stage_c(geom, x, cbz, zb, y_att, y_lru, mods, l, conv_w[l], w_out[l].astype(BF16), norm2_g[l],
                             None, n_experts)
            i = l // 2
            x = (_ffn(geom, x, h2, mods, l, ffn_w1[i].astype(BF16), ffn_w3[i].astype(BF16), ffn_w2[i].astype(BF16)),)
            if l == depth - 1:
                y_prompt = _final_norm(x[0], final_g, 0, geom.n_ctx)
                y_sample = _final_norm(x[0], final_g, geom.n_ctx, geom.n_lat)
        else:
            i = l // 2
            router_bf = jnp.zeros((d, 128), F32).at[:, :n_experts].set(router_w[i]).astype(BF16)
            x, h2p, route = _stage_c(geom, x, cbz, zb, y_att, y_lru, mods, l, conv_w[l], w_out[l].astype(BF16),
                                     norm2_g[l], router_bf, n_experts)
            pos, n_rows, plan = _moe_plan(route, n_experts)
            pos_flat = pos.reshape(-1)
            xs = _moe_scatter(h2p, pos_flat, n_rows)
            ys = _moe_gemm(xs, plan, moe_w1[i], moe_w3[i], moe_w2[i])
            if l == depth - 1:
                y_prompt = _moe_combine(geom, x, route, mods, l, final_g, ys, pos_flat, 0, geom.n_ctx)
                y_sample = _moe_combine(geom, x, route, mods, l, final_g, ys, pos_flat, geom.n_ctx, geom.n_lat)
            else:
                raise NotImplementedError("a routed layer is only supported as the last layer")

    return (y_prompt.reshape(batch, seq, d), y_sample.reshape(dec_batch, dec_seq, d),
            jnp.stack(new_k, axis=1), jnp.stack(new_v, axis=1), jnp.stack(new_lru, axis=1))
```

```python
import functools

import jax
import jax.numpy as jnp
from jax import lax
from jax.experimental import pallas as pl
from jax.experimental.pallas import tpu as pltpu

F32 = jnp.float32
BF16 = jnp.bfloat16

HEAD_DIM = 64
N_KV_HEADS = 2
GQA_GROUP = 4
GRID_W = 64
ROPE_BASE = 10000.0
WINDOW_BLOCK = 128
LRU_C = 8.0
LRU_SEGMENTS = 8
EPS = 1e-6
NEG = -1e30
LOG2E = 1.4426950408889634
TOP_K = 2

ROW_TILE = 512
MOE_ROW_TILE = 512
MOE_SUPER = 2048
MOE_FF_TILE = 512
VMEM_LIMIT = 52 * 1024 * 1024
MOE_GEMM_VMEM_LIMIT = 58 * 1024 * 1024


def _params(*sem):
    return pltpu.CompilerParams(dimension_semantics=sem, vmem_limit_bytes=VMEM_LIMIT)


def _mod_kernel(cond_ref, w_ref, b_ref, o_ref):
    c = cond_ref[...]
    s = c * jax.nn.sigmoid(c)
    o_ref[0] = jnp.dot(s.astype(BF16), w_ref[0].astype(BF16), preferred_element_type=F32) + b_ref[0]


def _modulation(cond, w_mod, b_mod):
    depth, d, d6 = w_mod.shape
    tn = d6 // 4
    return pl.pallas_call(
        _mod_kernel,
        grid=(depth, d6 // tn),
        in_specs=[pl.BlockSpec((8, d), lambda l, j: (0, 0)),
                  pl.BlockSpec((1, d, tn), lambda l, j: (l, 0, j)),
                  pl.BlockSpec((1, 1, tn), lambda l, j: (l, 0, j))],
        out_specs=pl.BlockSpec((1, 8, tn), lambda l, j: (l, 0, j)),
        out_shape=jax.ShapeDtypeStruct((depth, 8, d6), F32),
        compiler_params=_params("arbitrary", "arbitrary"),
        name="modulation",
    )(cond, w_mod, b_mod.reshape(depth, 1, d6))


class _Geom:
    def __init__(self, n_ctx_seq, ctx_len, n_lat_seq, lat_len):
        self.n_ctx_seq, self.ctx_len, self.n_lat_seq, self.lat_len = n_ctx_seq, ctx_len, n_lat_seq, lat_len
        self.n_ctx = n_ctx_seq * ctx_len
        self.n_lat = n_lat_seq * lat_len
        self.n_tok = self.n_ctx + self.n_lat
        assert self.n_ctx % ROW_TILE == 0 and lat_len % ROW_TILE == 0 and ROW_TILE % ctx_len == 0
        assert self.n_ctx % lat_len == 0 or self.n_ctx < lat_len
        self.ctx_tiles = self.n_ctx // ROW_TILE
        self.lat_tiles_per_seq = lat_len // ROW_TILE
        self.n_tiles = self.n_tok // ROW_TILE

    def mod_row(self, i):
        return jnp.where(i < self.ctx_tiles, 0, 1 + (i - self.ctx_tiles) // self.lat_tiles_per_seq)

    def rope_block(self, i):
        return jnp.where(i < self.ctx_tiles, 0, 1 + (i - self.ctx_tiles) % self.lat_tiles_per_seq)


def _rms_mod(x, g, shift, scale):
    y = x * lax.rsqrt(jnp.mean(x * x, axis=-1, keepdims=True) + EPS)
    return (y * g) * (1 + scale) + shift


def _rope(x, cos, sin):
    lane = lax.broadcasted_iota(jnp.int32, x.shape, 1)
    swapped = jnp.where((lane & 31) < 16, pltpu.roll(x, 128 - 16, 1), pltpu.roll(x, 16, 1))
    return x * cos + swapped * sin


def _stream_specs(geom, x_parts, d):
    tm = ROW_TILE
    if len(x_parts) == 1:
        return [pl.BlockSpec((tm, d), lambda i, *_: (i, 0))]
    ct = geom.ctx_tiles
    return [pl.BlockSpec((tm, d), lambda i, *_: (jnp.minimum(i, ct - 1), 0)),
            pl.BlockSpec((tm, d), lambda i, *_: (jnp.maximum(i - ct, 0), 0))]


def _stream_tile(x_refs, ctx_tiles):
    if len(x_refs) == 1:
        return x_refs[0][...]
    return jnp.where(pl.program_id(0) < ctx_tiles, x_refs[0][...], x_refs[1][...])


def _stage_a_kernel(*refs, n_x, ctx_tiles, cw, aw, kw):
    x_refs = refs[:n_x]
    mod_ref, g_ref, w_ref, cos_ref, sin_ref, cbz_ref, zb_ref, q_ref, k_ref, v_ref, lru_ref = refs[n_x:]
    m = mod_ref[0, 0]
    h = _rms_mod(_stream_tile(x_refs, ctx_tiles), g_ref[...], m[0:1], m[1:2]).astype(BF16)
    rows = h.shape[0]

    def proj(lo, hi):
        return jnp.dot(h, w_ref[:, lo:hi], preferred_element_type=F32)

    cb = proj(0, cw)
    z = proj(cw, 2 * cw) * proj(2 * cw, 3 * cw)
    cbz_ref[:, 0:cw] = cb
    cbz_ref[:, cw:2 * cw] = z
    zb_ref[0, 0:8] = z[0:8]
    zb_ref[0, 8:16] = z[rows - 8:rows]

    cos, sin = cos_ref[...], sin_ref[...]
    o = 3 * cw
    for c in range(aw // 128):
        qc = _rope(proj(o + c * 128, o + (c + 1) * 128), cos, sin)
        q_ref[:, c * 128:(c + 1) * 128] = (qc * (HEAD_DIM ** -0.5 * LOG2E)).astype(BF16)
    o += aw
    k_ref[...] = _rope(proj(o, o + kw), cos, sin)
    v_ref[...] = proj(o + kw, o + 2 * kw)
    o += 2 * kw
    lru_ref[...] = proj(o, o + 2 * cw)


def _stage_a(geom, x_parts, mods, l, norm_g, w_in_bf, cos_t, sin_t, cw, aw, kw):
    n, d = geom.n_tok, x_parts[0].shape[1]
    tm = ROW_TILE
    d_in = w_in_bf.shape[1]
    row = lambda i: (i, 0)
    out_shape = (jax.ShapeDtypeStruct((n, 2 * cw), F32),
                 jax.ShapeDtypeStruct((geom.n_tiles, 16, cw), F32),
                 jax.ShapeDtypeStruct((n, aw), BF16),
                 jax.ShapeDtypeStruct((n, kw), F32),
                 jax.ShapeDtypeStruct((n, kw), F32),
                 jax.ShapeDtypeStruct((n, 2 * cw), F32))
    return pl.pallas_call(
        functools.partial(_stage_a_kernel, n_x=len(x_parts), ctx_tiles=geom.ctx_tiles, cw=cw, aw=aw, kw=kw),
        grid=(geom.n_tiles,),
        in_specs=_stream_specs(geom, x_parts, d) + [
                  pl.BlockSpec((1, 1, 6, d), lambda i: (l, geom.mod_row(i), 0, 0)),
                  pl.BlockSpec((1, d), lambda i: (0, 0)),
                  pl.BlockSpec((d, d_in), lambda i: (0, 0)),
                  pl.BlockSpec((tm, 128), lambda i: (geom.rope_block(i), 0)),
                  pl.BlockSpec((tm, 128), lambda i: (geom.rope_block(i), 0))],
        out_specs=(pl.BlockSpec((tm, 2 * cw), row),
                   pl.BlockSpec((1, 16, cw), lambda i: (i, 0, 0)),
                   pl.BlockSpec((tm, aw), row),
                   pl.BlockSpec((tm, kw), row),
                   pl.BlockSpec((tm, kw), row),
                   pl.BlockSpec((tm, 2 * cw), row)),
        out_shape=out_shape,
        compiler_params=_params("arbitrary"),
        name=f"stage_a_l{l}",
    )(*x_parts, mods, norm_g.reshape(1, d), w_in_bf, cos_t, sin_t)


def _qk(q, k):
    return lax.dot_general(q, k, (((1,), (1,)), ((), ())), preferred_element_type=F32)


def _attn_kernel(*refs, windowed, nb):
    n_in = 10 if windowed else 4
    ins, rest = refs[:n_in], refs[n_in:]
    if windowed:
        q_ref, kl_ref, km_ref, kr_ref, vl_ref, vm_ref, vr_ref, kc_ref, vc_ref, sink_ref = ins
    else:
        q_ref, km_ref, vm_ref, sink_ref = ins
    o_ref, ka_ref, kb_ref, va_ref, vb_ref = rest
    tq = q_ref.shape[0]

    def fill(r0, k, v):
        n = k.shape[0]
        lane = lax.broadcasted_iota(jnp.int32, (n, 128), 1)
        low = lane < HEAD_DIM
        k_sw, v_sw = pltpu.roll(k, HEAD_DIM, 1), pltpu.roll(v, HEAD_DIM, 1)
        one_hi = jnp.where(lane == HEAD_DIM, 1.0, 0.0)
        one_lo = jnp.where(lane == 0, 1.0, 0.0)
        rows = pl.ds(r0, n)
        for g, (kg_lo, kg_hi, vg_lo, vg_hi) in enumerate(((k, k_sw, v, v_sw), (k_sw, k, v_sw, v))):
            ka_ref[g, rows, :] = jnp.where(low, kg_lo, 0.0).astype(BF16)
            kb_ref[g, rows, :] = jnp.where(low, 0.0, kg_hi).astype(BF16)
            va_ref[g, rows, :] = jnp.where(low, vg_lo, one_hi).astype(BF16)
            vb_ref[g, rows, :] = jnp.where(low, one_lo, vg_hi).astype(BF16)

    if windowed:
        j = pl.program_id(1)
        fill(0, kl_ref[...], vl_ref[...])
        fill(tq, km_ref[...], vm_ref[...])
        fill(2 * tq, kr_ref[...], vr_ref[...])

        @pl.when(j == 0)
        def _():
            fill(3 * tq, kc_ref[0, 0], vc_ref[0, 0])

        r = lax.broadcasted_iota(jnp.int32, (2 * tq, tq), 0) & (tq - 1)
        c = lax.broadcasted_iota(jnp.int32, (2 * tq, tq), 1)
        left_ok = jnp.logical_and(c >= r, j > 0)
        right_ok = jnp.logical_and(c <= r, j < nb - 1)
    else:
        fill(0, km_ref[...], vm_ref[...])

    sink_all = sink_ref[...] * LOG2E
    lane = lax.broadcasted_iota(jnp.int32, (2 * tq, 128), 1)
    for g in range(N_KV_HEADS):
        q2 = jnp.concatenate([q_ref[:, (2 * g) * 128:(2 * g + 1) * 128],
                              q_ref[:, (2 * g + 1) * 128:(2 * g + 2) * 128]], axis=0)
        res = []
        for odd, (k_ref, v_ref, den_lane) in enumerate(((ka_ref, va_ref, HEAD_DIM), (kb_ref, vb_ref, 0))):
            s = _qk(q2, k_ref[g])
            if windowed:
                s = jnp.concatenate([jnp.where(left_ok, s[:, 0:tq], NEG), s[:, tq:2 * tq],
                                     jnp.where(right_ok, s[:, 2 * tq:3 * tq], NEG), s[:, 3 * tq:]], axis=1)
            h0 = GQA_GROUP * g + odd
            sink_col = jnp.concatenate([jnp.broadcast_to(sink_all[h0:h0 + 1, 0:1], (tq, 1)),
                                        jnp.broadcast_to(sink_all[h0 + 2:h0 + 3, 0:1], (tq, 1))], axis=0)
            m = jnp.maximum(jnp.max(s, axis=-1, keepdims=True), sink_col)
            o = jnp.dot(jnp.exp2(s - m).astype(BF16), v_ref[g], preferred_element_type=F32)
            res.append(o / (o[:, den_lane:den_lane + 1] + jnp.exp2(sink_col - m)))
        merged = jnp.where(lane < HEAD_DIM, res[0], res[1]).astype(BF16)
        o_ref[:, (2 * g) * 128:(2 * g + 1) * 128] = merged[0:tq]
        o_ref[:, (2 * g + 1) * 128:(2 * g + 2) * 128] = merged[tq:2 * tq]


def _attn_scratch(n_keys):
    return [pltpu.VMEM((N_KV_HEADS, n_keys, 128), BF16)] * 4


def _attention_ctx(geom, q, k, v, sink_b):
    t, aw, kw = geom.ctx_len, q.shape[1], k.shape[1]
    assert kw == 128 and aw == 2 * N_KV_HEADS * 128
    blk = lambda b: (b, 0)
    return pl.pallas_call(
        functools.partial(_attn_kernel, windowed=False, nb=1),
        grid=(geom.n_ctx_seq,),
        in_specs=[pl.BlockSpec((t, aw), blk), pl.BlockSpec((t, kw), blk), pl.BlockSpec((t, kw), blk),
                  pl.BlockSpec(sink_b.shape, lambda b: (0, 0))],
        out_specs=pl.BlockSpec((t, aw), blk),
        out_shape=jax.ShapeDtypeStruct((geom.n_ctx, aw), BF16),
        scratch_shapes=_attn_scratch(t),
        compiler_params=_params("arbitrary"),
        name="attention_ctx",
    )(q, k, v, sink_b)


def _attention_lat(geom, q, k, v, cache_k, cache_v, l, sink_b):
    aw, kw = q.shape[1], k.shape[1]
    tq = WINDOW_BLOCK
    nb = geom.lat_len // tq
    off = geom.n_ctx // tq
    past = cache_k.shape[2]

    def at(delta):
        return lambda b, j: (off + b * nb + jnp.clip(j + delta, 0, nb - 1), 0)

    kv_spec = [pl.BlockSpec((tq, kw), at(-1)), pl.BlockSpec((tq, kw), at(0)), pl.BlockSpec((tq, kw), at(1))]
    cache_spec = pl.BlockSpec((1, 1, past, kw), lambda b, j: (b, l, 0, 0))
    return pl.pallas_call(
        functools.partial(_attn_kernel, windowed=True, nb=nb),
        grid=(geom.n_lat_seq, nb),
        in_specs=[pl.BlockSpec((tq, aw), at(0))] + kv_spec + kv_spec + [cache_spec, cache_spec,
                  pl.BlockSpec(sink_b.shape, lambda b, j: (0, 0))],
        out_specs=pl.BlockSpec((tq, aw), lambda b, j: (b * nb + j, 0)),
        out_shape=jax.ShapeDtypeStruct((geom.n_lat, aw), BF16),
        scratch_shapes=_attn_scratch(3 * tq + past),
        compiler_params=_params("arbitrary", "arbitrary"),
        name="attention_lat",
    )(q, k, k, k, v, v, v, cache_k, cache_v, sink_b)


def _log_sigmoid(x):
    return jnp.minimum(x, 0.0) - jnp.log(1.0 + jnp.exp(-jnp.abs(x)))


def _lru_kernel(x_ref, h0_ref, cw_ref, cb_ref, wg_ref, bg_ref, lam_ref, y_ref, ht_ref,
                xp_ref, af_ref, uf_ref, ab_ref, ub_ref, hin_ref, *, t, lw):
    seg = LRU_SEGMENTS
    seg_len = t // seg
    tg = min(t, 256)
    tc = min(seg_len, 256)
    nh = lw // 128

    def put(ref, r0, rows, val):
        for hh in range(nh):
            ref[hh, pl.ds(r0, rows), :] = val[:, hh * 128:(hh + 1) * 128]

    def get(ref, r0, rows):
        return jnp.concatenate([ref[hh, pl.ds(r0, rows), :] for hh in range(nh)], axis=1)

    zeros8 = jnp.zeros((8, lw), F32)
    xp_ref[0:8] = zeros8
    xp_ref[t + 8:t + 16] = zeros8
    xp_ref[8:t + 8] = x_ref[:, 0:lw]

    dirs = ((af_ref, uf_ref), (ab_ref, ub_ref))
    log_sig = [_log_sigmoid(lam_ref[d]) for d in range(2)]
    taps = cw_ref.shape[1]

    def gate_chunk(c, carry):
        r0 = pl.multiple_of(c * tg, tg)
        win = xp_ref[pl.ds(r0, tg + 16), :]
        for d, (a_ref, u_ref) in enumerate(dirs):
            cw = cw_ref[d]
            xc = None
            for k in range(taps):
                s0 = 8 - (taps - 1) + k if d == 0 else 8 + k
                term = win[s0:s0 + tg] * cw[k:k + 1]
                xc = term if xc is None else xc + term
            xc = xc + cb_ref[d]
            gates = jnp.dot(xc.astype(BF16), wg_ref[d], preferred_element_type=F32) + bg_ref[d]
            r = 0.5 * jnp.tanh(0.5 * gates[:, 0:lw]) + 0.5
            i = 0.5 * jnp.tanh(0.5 * gates[:, lw:2 * lw]) + 0.5
            log_a = (LRU_C * r) * log_sig[d]
            a = jnp.exp(log_a)
            u = jnp.sqrt(-jnp.tanh(log_a) * (a * a + 1.0)) * (i * xc)
            put(a_ref, r0, tg, a)
            put(u_ref, r0, tg, u)
        return carry

    lax.fori_loop(0, t // tg, gate_chunk, 0)

    def scan_step(jj, carry):
        out = []
        for d, (a_ref, u_ref) in enumerate(dirs):
            j = jj if d == 0 else seg_len - 1 - jj
            idx = pl.ds(j, seg, stride=seg_len)
            for hh in range(nh):
                p, u = carry[len(out)], carry[len(out) + 1]
                a = a_ref[hh, idx, :]
                p = a * p
                u = a * u + u_ref[hh, idx, :]
                a_ref[hh, idx, :] = p
                u_ref[hh, idx, :] = u
                out += [p, u]
        return tuple(out)

    init = (jnp.ones((seg, 128), F32), jnp.zeros((seg, 128), F32)) * (2 * nh)
    ends = lax.fori_loop(0, seg_len, scan_step, init, unroll=4)

    for d in range(2):
        p_end = jnp.concatenate(ends[2 * nh * d:2 * nh * (d + 1):2], axis=1)
        u_end = jnp.concatenate(ends[2 * nh * d + 1:2 * nh * (d + 1):2], axis=1)
        hcur = h0_ref[0, d:d + 1, :]
        for s in (range(seg) if d == 0 else range(seg - 1, -1, -1)):
            hin_ref[d, s:s + 1, :] = hcur
            hcur = p_end[s:s + 1] * hcur + u_end[s:s + 1]
        ht_ref[0, d:d + 1, :] = hcur

    def out_chunk(c, carry):
        r0 = pl.multiple_of(c * tc, tc)
        s = r0 // seg_len
        hf = get(af_ref, r0, tc) * hin_ref[0, pl.ds(s, 1), :] + get(uf_ref, r0, tc)
        hb = get(ab_ref, r0, tc) * hin_ref[1, pl.ds(s, 1), :] + get(ub_ref, r0, tc)
        lg = x_ref[pl.ds(r0, tc), lw:2 * lw]
        y_ref[pl.ds(r0, tc), :] = ((hf + hb) * jax.nn.gelu(lg)).astype(BF16)
        return carry

    lax.fori_loop(0, t // tc, out_chunk, 0)


def _lru(lru_in, row_off, n_seq, t, h0, lru_p):
    conv_w, conv_b, w_gate, b_gate, lam = lru_p
    lw = lru_in.shape[1] // 2
    assert row_off % t == 0 and t % (8 * LRU_SEGMENTS) == 0
    off = row_off // t
    whole = lambda a: pl.BlockSpec(a.shape, lambda b: (0,) * a.ndim)
    return pl.pallas_call(
        functools.partial(_lru_kernel, t=t, lw=lw),
        grid=(n_seq,),
        in_specs=[pl.BlockSpec((t, 2 * lw), lambda b: (off + b, 0)),
                  pl.BlockSpec((1, 2, lw), lambda b: (b, 0, 0)),
                  whole(conv_w), whole(conv_b), whole(w_gate), whole(b_gate), whole(lam)],
        out_specs=(pl.BlockSpec((t, lw), lambda b: (b, 0)),
                   pl.BlockSpec((1, 2, lw), lambda b: (b, 0, 0))),
        out_shape=(jax.ShapeDtypeStruct((n_seq * t, lw), BF16),
                   jax.ShapeDtypeStruct((n_seq, 2, lw), F32)),
        scratch_shapes=[pltpu.VMEM((t + 16, lw), F32)] + [pltpu.VMEM((lw // 128, t, 128), F32)] * 4
                       + [pltpu.VMEM((2, LRU_SEGMENTS, lw), F32)],
        compiler_params=_params("arbitrary"),
        name=f"rglru_t{t}",
    )(lru_in, h0, conv_w, conv_b, w_gate, b_gate, lam)


def _route_top2(h2b, router_ref, n_experts):
    logits = jnp.dot(h2b, router_ref[...], preferred_element_type=F32)
    lane = lax.broadcasted_iota(jnp.int32, logits.shape, 1).astype(F32)
    lg = jnp.where(lane < n_experts, logits, -jnp.inf)
    v1 = jnp.max(lg, axis=-1, keepdims=True)
    e1 = jnp.min(jnp.where(lg == v1, lane, 1e9), axis=-1, keepdims=True)
    lg2 = jnp.where(lane == e1, -jnp.inf, lg)
    v2 = jnp.max(lg2, axis=-1, keepdims=True)
    e2 = jnp.min(jnp.where(lg2 == v2, lane, 1e9), axis=-1, keepdims=True)
    ex = jnp.exp(v2 - v1)
    den = 1.0 + ex
    g1 = 1.0 / den
    g2 = ex / den
    return jnp.where(lane == 0, e1, jnp.where(lane == 1, e2, jnp.where(lane == 2, g1, jnp.where(lane == 3, g2, 0.0))))


def _stage_c_kernel(*refs, n_x, geom, cw, aw, routed, n_experts):
    x_refs, att_refs, lru_refs, refs = refs[:n_x], refs[n_x:n_x + 2], refs[n_x + 2:n_x + 4], refs[n_x + 4:]
    if routed:
        (cbz_ref, zprev_ref, znext_ref, mod_ref, convw_ref, wo_ref, g_ref, router_ref,
         xo_ref, h2_ref, route_ref) = refs
    else:
        (cbz_ref, zprev_ref, znext_ref, mod_ref, convw_ref, wo_ref, g_ref, xo_ref, h2_ref) = refs
    i = pl.program_id(0)
    tm = cbz_ref.shape[0]
    cb = cbz_ref[:, 0:cw]
    z = cbz_ref[:, cw:2 * cw]
    row = lax.broadcasted_iota(jnp.int32, (tm, 1), 0)
    seq_len = jnp.where(i < geom.ctx_tiles, geom.ctx_len, geom.lat_len)
    tpos = (i * tm + row) & (seq_len - 1)
    z_dn = jnp.where(row == 0, zprev_ref[0, 15:16], pltpu.roll(z, 1, 0))
    z_up = jnp.where(row == tm - 1, znext_ref[0, 0:1], pltpu.roll(z, tm - 1, 0))
    z_dn = jnp.where(tpos == 0, 0.0, z_dn)
    z_up = jnp.where(tpos == seq_len - 1, 0.0, z_up)
    cwt = convw_ref[...]
    y_conv = cb * (z_dn * cwt[0:1] + z * cwt[1:2] + z_up * cwt[2:3])

    mix = (jnp.dot(y_conv.astype(BF16), wo_ref[0:cw], preferred_element_type=F32)
           + jnp.dot(_stream_tile(att_refs, geom.ctx_tiles), wo_ref[cw:cw + aw], preferred_element_type=F32)
           + jnp.dot(_stream_tile(lru_refs, geom.ctx_tiles), wo_ref[cw + aw:2 * cw + aw],
                     preferred_element_type=F32))
    m = mod_ref[0, 0]
    xn = _stream_tile(x_refs, geom.ctx_tiles) + m[2:3] * mix
    xo_ref[...] = xn
    h2 = _rms_mod(xn, g_ref[...], m[3:4], m[4:5])
    if routed:
        h2_ref[...] = h2
        route_ref[...] = _route_top2(h2.astype(BF16), router_ref, n_experts)
    else:
        h2_ref[...] = h2.astype(BF16)


def _stage_c(geom, x_parts, cbz, zb, y_att, y_lru, mods, l, conv_w, w_out_bf, norm_g, router_bf, n_experts):
    n, d = geom.n_tok, x_parts[0].shape[1]
    tm = ROW_TILE
    cw, aw = conv_w.shape[1], y_att[0].shape[1]
    routed = router_bf is not None
    row = lambda i: (i, 0)
    last = geom.n_tiles - 1
    in_specs = _stream_specs(geom, x_parts, d) + _stream_specs(geom, y_att, aw) + _stream_specs(geom, y_lru, cw) + [
                pl.BlockSpec((tm, 2 * cw), row),
                pl.BlockSpec((1, 16, cw), lambda i: (jnp.maximum(i - 1, 0), 0, 0)),
                pl.BlockSpec((1, 16, cw), lambda i: (jnp.minimum(i + 1, last), 0, 0)),
                pl.BlockSpec((1, 1, 6, d), lambda i: (l, geom.mod_row(i), 0, 0)),
                pl.BlockSpec(conv_w.shape, lambda i: (0, 0)),
                pl.BlockSpec(w_out_bf.shape, lambda i: (0, 0)),
                pl.BlockSpec((1, d), lambda i: (0, 0))]
    args = [*x_parts, *y_att, *y_lru, cbz, zb, zb, mods, conv_w, w_out_bf, norm_g.reshape(1, d)]
    out_specs = [pl.BlockSpec((tm, d), row)]
    out_shape = [jax.ShapeDtypeStruct((n, d), F32)]
    if routed:
        in_specs.append(pl.BlockSpec(router_bf.shape, lambda i: (0, 0)))
        args.append(router_bf)
        out_specs += [pl.BlockSpec((tm, d), row), pl.BlockSpec((tm, 128), row)]
        out_shape += [jax.ShapeDtypeStruct((n, d), F32), jax.ShapeDtypeStruct((n, 128), F32)]
    else:
        out_specs.append(pl.BlockSpec((tm, d), row))
        out_shape.append(jax.ShapeDtypeStruct((n, d), BF16))
    return pl.pallas_call(
        functools.partial(_stage_c_kernel, n_x=len(x_parts), geom=geom, cw=cw, aw=aw, routed=routed,
                          n_experts=n_experts),
        grid=(geom.n_tiles,),
        in_specs=in_specs, out_specs=tuple(out_specs), out_shape=tuple(out_shape),
        compiler_params=_params("arbitrary"),
        name=f"stage_c_l{l}",
    )(*args)


def _silu(x):
    return x * jax.nn.sigmoid(x)


def _ffn_kernel(x_ref, h_ref, mod_ref, w1_ref, w3_ref, w2_ref, o_ref, *, fc):
    h = h_ref[...]
    ff = w1_ref.shape[1]
    acc = None
    for c in range(ff // fc):
        sl = slice(c * fc, (c + 1) * fc)
        a = _silu(jnp.dot(h, w1_ref[:, sl], preferred_element_type=F32)) * jnp.dot(
            h, w3_ref[:, sl], preferred_element_type=F32)
        part = jnp.dot(a.astype(BF16), w2_ref[sl, :], preferred_element_type=F32)
        acc = part if acc is None else acc + part
    o_ref[...] = x_ref[...] + mod_ref[0, 0][5:6] * acc


def _ffn(geom, x, h2, mods, l, w1, w3, w2):
    n, d = x.shape
    tm = ROW_TILE
    ff = w1.shape[1]
    fc = 256
    assert ff % fc == 0
    row = lambda i: (i, 0)
    const = lambda a: pl.BlockSpec(a.shape, lambda i: (0, 0), pipeline_mode=pl.Buffered(1))
    return pl.pallas_call(
        functools.partial(_ffn_kernel, fc=fc),
        grid=(geom.n_tiles,),
        in_specs=[pl.BlockSpec((tm, d), row), pl.BlockSpec((tm, d), row),
                  pl.BlockSpec((1, 1, 6, d), lambda i: (l, geom.mod_row(i), 0, 0)),
                  const(w1), const(w3), const(w2)],
        out_specs=pl.BlockSpec((tm, d), row),
        out_shape=jax.ShapeDtypeStruct((n, d), F32),
        compiler_params=_params("arbitrary"),
        name=f"ffn_l{l}",
    )(x, h2, mods, w1, w3, w2)


def _moe_plan(route, n_experts):
    n = route.shape[0]
    tmg, sup = MOE_ROW_TILE, MOE_SUPER
    i32 = jnp.int32
    experts = route[:, 0:TOP_K].astype(i32)
    onehot = (experts[:, :, None] == jnp.arange(n_experts, dtype=i32)[None, None, :]).astype(i32)
    sel = jnp.sum(onehot, axis=1)
    incl = jnp.cumsum(sel, axis=0)
    rank = incl - sel
    counts = incl[-1]
    tiles = (counts + tmg - 1) // tmg
    tile_end = jnp.cumsum(tiles)
    tile_off = tile_end - tiles
    pos = jnp.sum(onehot * ((tile_off * tmg)[None, :] + rank)[:, None, :], axis=2)

    n_rows = (TOP_K * n + n_experts * (tmg - 1)) // tmg * tmg
    n_rows = (n_rows + sup - 1) // sup * sup
    n_super = n_rows // sup
    per = sup // tmg
    n_work = n_super + n_experts - 1
    used = tile_end[-1]
    last_sup = (used - 1) // per
    end_ext = jnp.where((tile_end == used) & (tiles > 0), (last_sup + 1) * per, tile_end)
    s0 = (jnp.arange(n_super, dtype=i32) * per)[:, None]
    lo = jnp.clip(tile_off[None, :] - s0, 0, per).reshape(-1)
    hi = jnp.clip(end_ext[None, :] - s0, 0, per).reshape(-1)
    live = ((hi > lo) & jnp.tile(tiles > 0, n_super)).astype(i32)
    slot = jnp.cumsum(live) - live
    n_live = jnp.sum(live)
    w = jnp.arange(n_work, dtype=i32)
    hit = (slot[None, :] == jnp.minimum(w, n_live - 1)[:, None]) & (live[None, :] > 0)

    def pick(v):
        return jnp.sum(jnp.where(hit, v[None, :], 0), axis=1).astype(i32)

    pair = jnp.arange(n_super * n_experts, dtype=i32)
    busy = w < n_live
    n_tiles = n_rows // tmg
    tail = jnp.arange(n_tiles - (TOP_K * n + tmg - 1) // tmg, dtype=i32)
    zero_tiles = jnp.concatenate([jnp.maximum(tile_end - 1, 0), jnp.minimum(used + tail, n_tiles - 1)]).astype(i32)
    return pos.astype(i32), n_rows, zero_tiles, (pick(pair // n_experts), pick(pair % n_experts),
                                                 jnp.where(busy, pick(lo), 0), jnp.where(busy, pick(hi), 0))


def _moe_scatter_kernel(pos_ref, zt_ref, h_ref, xs_ref, zero_ref, sem, zsem, *, tile_off):
    i = pl.program_id(0)
    tm = h_ref.shape[0]
    base = (tile_off + i) * tm * TOP_K

    @pl.when(i == 0)
    def _():
        zero_ref[...] = jnp.zeros(zero_ref.shape, F32)
        for t in range(zt_ref.shape[0]):
            z = pltpu.make_async_copy(zero_ref, xs_ref.at[pl.ds(pl.multiple_of(zt_ref[t] * tm, tm), tm)], zsem)
            z.start()
            z.wait()

    def copy(r, k):
        return pltpu.make_async_copy(h_ref.at[pl.ds(r, 1)], xs_ref.at[pl.ds(pos_ref[base + r * TOP_K + k], 1)], sem)

    def start(r, c):
        for k in range(TOP_K):
            copy(r, k).start(priority=k)
        return c

    def wait(r, c):
        for k in range(TOP_K):
            copy(r, k).wait()
        return c

    lax.fori_loop(0, tm, start, 0, unroll=8)
    lax.fori_loop(0, tm, wait, 0, unroll=8)


def _moe_scatter(h2p, pos_flat, zero_tiles, n_rows):
    n, w = h2p.shape
    tm = ROW_TILE
    assert tm == MOE_ROW_TILE
    return pl.pallas_call(
        functools.partial(_moe_scatter_kernel, tile_off=0),
        grid_spec=pltpu.PrefetchScalarGridSpec(
            num_scalar_prefetch=2,
            grid=(n // tm,),
            in_specs=[pl.BlockSpec((tm, w), lambda i, pos, zt: (i, 0))],
            out_specs=pl.BlockSpec(memory_space=pl.ANY),
            scratch_shapes=[pltpu.VMEM((tm, w), F32), pltpu.SemaphoreType.DMA(()), pltpu.SemaphoreType.DMA(())]),
        out_shape=jax.ShapeDtypeStruct((n_rows, w), F32),
        compiler_params=_params("arbitrary"),
        name="moe_scatter",
    )(pos_flat, zero_tiles, h2p)


def _moe_gemm_kernel(wsup_ref, wexp_ref, lo_ref, hi_ref, xs_ref, w1_ref, w3_ref, w2_ref, o_ref,
                     w1b_ref, w3b_ref, w2b_ref):
    w = pl.program_id(0)
    j = pl.program_id(1)
    tmg = MOE_ROW_TILE
    lo, hi = lo_ref[w], hi_ref[w]
    fc = 256

    @pl.when(hi > lo)
    def _():
        w1b_ref[...] = w1_ref[0].astype(BF16)
        w3b_ref[...] = w3_ref[0].astype(BF16)
        w2b_ref[...] = w2_ref[0].astype(BF16)

        def sub(s, carry, first):
            r0 = pl.multiple_of(s * tmg, tmg)
            xb = xs_ref[pl.ds(r0, tmg), :].astype(BF16)
            part = None
            for c in range(w1b_ref.shape[1] // fc):
                sl = slice(c * fc, (c + 1) * fc)
                h1 = jnp.dot(xb, w1b_ref[:, sl], preferred_element_type=F32)
                h3 = jnp.dot(xb, w3b_ref[:, sl], preferred_element_type=F32)
                pc = jnp.dot((_silu(h1) * h3).astype(BF16), w2b_ref[sl, :], preferred_element_type=F32)
                part = pc if part is None else part + pc
            if first:
                o_ref[pl.ds(r0, tmg), :] = part
            else:
                o_ref[pl.ds(r0, tmg), :] += part
            return carry

        @pl.when(j == 0)
        def _():
            lax.fori_loop(lo, hi, functools.partial(sub, first=True), 0)

        @pl.when(j > 0)
        def _():
            lax.fori_loop(lo, hi, functools.partial(sub, first=False), 0)


def _moe_gemm(xs, plan, w1, w3, w2):
    n_rows, wp = xs.shape
    n_exp, d, ff = w1.shape
    sup, tf = MOE_SUPER, MOE_FF_TILE
    w_sup, w_exp, lo, hi = plan
    n_work = w_sup.shape[0]
    return pl.pallas_call(
        _moe_gemm_kernel,
        grid_spec=pltpu.PrefetchScalarGridSpec(
            num_scalar_prefetch=4,
            grid=(n_work, ff // tf),
            in_specs=[pl.BlockSpec((sup, wp), lambda w, j, ws, we, l_, h_: (ws[w], 0)),
                      pl.BlockSpec((1, d, tf), lambda w, j, ws, we, l_, h_: (we[w], 0, j)),
                      pl.BlockSpec((1, d, tf), lambda w, j, ws, we, l_, h_: (we[w], 0, j)),
                      pl.BlockSpec((1, tf, d), lambda w, j, ws, we, l_, h_: (we[w], j, 0))],
            out_specs=pl.BlockSpec((sup, d), lambda w, j, ws, we, l_, h_: (ws[w], 0)),
            scratch_shapes=[pltpu.VMEM((d, tf), BF16), pltpu.VMEM((d, tf), BF16), pltpu.VMEM((tf, d), BF16)]),
        out_shape=jax.ShapeDtypeStruct((n_rows, d), F32),
        compiler_params=pltpu.CompilerParams(dimension_semantics=("arbitrary", "arbitrary"),
                                             vmem_limit_bytes=MOE_GEMM_VMEM_LIMIT),
        name="moe_gemm",
    )(w_sup, w_exp, lo, hi, xs, w1, w3, w2)


def _moe_combine_kernel(pos_ref, x_ref, route_ref, mod_ref, g_ref, ys_ref, o_ref, buf_ref, sem,
                        *, tile_off, n_tiles):
    i = pl.program_id(0)
    tm = x_ref.shape[0]

    def copy(tile, slot, r, k):
        base = (tile_off + tile) * tm * TOP_K
        return pltpu.make_async_copy(ys_ref.at[pl.ds(pos_ref[base + r * TOP_K + k], 1)],
                                     buf_ref.at[slot, k, pl.ds(r, 1)], sem.at[slot])

    def issue(tile, slot):
        def start(r, c):
            for k in range(TOP_K):
                copy(tile, slot, r, k).start(priority=k)
            return c

        lax.fori_loop(0, tm, start, 0, unroll=8)

    @pl.when(i == 0)
    def _():
        issue(0, 0)

    @pl.when(i + 1 < n_tiles)
    def _():
        issue(i + 1, (i + 1) % 2)

    slot = i % 2

    def wait(r, c):
        for k in range(TOP_K):
            copy(i, slot, r, k).wait()
        return c

    lax.fori_loop(0, tm, wait, 0, unroll=8)
    route = route_ref[...]
    moe = route[:, 2:3] * buf_ref[slot, 0] + route[:, 3:4] * buf_ref[slot, 1]
    xn = x_ref[...] + mod_ref[0, 0][5:6] * moe
    y = xn * lax.rsqrt(jnp.mean(xn * xn, axis=-1, keepdims=True) + EPS)
    o_ref[...] = y * g_ref[...]


def _moe_combine(geom, x, route, mods, l, final_g, ys, pos_flat, row_off, n_out):
    d = x.shape[1]
    tm = ROW_TILE
    t0 = row_off // tm
    return pl.pallas_call(
        functools.partial(_moe_combine_kernel, tile_off=t0, n_tiles=n_out // tm),
        grid_spec=pltpu.PrefetchScalarGridSpec(
            num_scalar_prefetch=1,
            grid=(n_out // tm,),
            in_specs=[pl.BlockSpec((tm, d), lambda i, pos: (t0 + i, 0)),
                      pl.BlockSpec((tm, 128), lambda i, pos: (t0 + i, 0)),
                      pl.BlockSpec((1, 1, 6, d), lambda i, pos: (l, geom.mod_row(t0 + i), 0, 0)),
                      pl.BlockSpec((1, d), lambda i, pos: (0, 0)),
                      pl.BlockSpec(memory_space=pl.ANY)],
            out_specs=pl.BlockSpec((tm, d), lambda i, pos: (i, 0)),
            scratch_shapes=[pltpu.VMEM((2, TOP_K, tm, d), F32), pltpu.SemaphoreType.DMA((2,))]),
        out_shape=jax.ShapeDtypeStruct((n_out, d), F32),
        compiler_params=_params("arbitrary"),
        name="moe_combine",
    )(pos_flat, x, route, mods, final_g.reshape(1, d), ys)


def _final_norm_kernel(x_ref, g_ref, o_ref):
    x = x_ref[...]
    o_ref[...] = (x * lax.rsqrt(jnp.mean(x * x, axis=-1, keepdims=True) + EPS)) * g_ref[...]


def _final_norm(x, final_g, row_off, n_out):
    d = x.shape[1]
    tm = ROW_TILE
    t0 = row_off // tm
    return pl.pallas_call(
        _final_norm_kernel,
        grid=(n_out // tm,),
        in_specs=[pl.BlockSpec((tm, d), lambda i: (t0 + i, 0)), pl.BlockSpec((1, d), lambda i: (0, 0))],
        out_specs=pl.BlockSpec((tm, d), lambda i: (i, 0)),
        out_shape=jax.ShapeDtypeStruct((n_out, d), F32),
        compiler_params=_params("arbitrary"),
        name="final_norm",
    )(x, final_g.reshape(1, d))


def _rope_tables(t):
    pairs = HEAD_DIM // 4
    pos = jnp.arange(t)
    inv = 1.0 / (ROPE_BASE ** (jnp.arange(pairs, dtype=F32) / pairs))
    ar = (pos // GRID_W).astype(F32)[:, None] * inv
    ac = (pos % GRID_W).astype(F32)[:, None] * inv
    cos = jnp.concatenate([jnp.cos(ar), jnp.cos(ar), jnp.cos(ac), jnp.cos(ac)], axis=1)
    sin = jnp.concatenate([-jnp.sin(ar), jnp.sin(ar), -jnp.sin(ac), jnp.sin(ac)], axis=1)
    reps = 128 // HEAD_DIM
    cos = jnp.concatenate([jnp.ones((ROW_TILE, 128), F32), jnp.tile(cos, (1, reps))], axis=0)
    sin = jnp.concatenate([jnp.zeros((ROW_TILE, 128), F32), jnp.tile(sin, (1, reps))], axis=0)
    return cos, sin


def _block_diag(w):
    n, b, _ = w.shape
    eye = jnp.eye(n, dtype=w.dtype)
    return (eye[:, None, :, None] * w[:, :, None, :]).reshape(n * b, n * b)


def kernel(x_prompt, x_sample, cache_k, cache_v, state_lru, c, c_ctx, norm1_g, norm2_g, w_mod, b_mod, w_in,
           conv_w, attn_sink, lru_conv_w, lru_conv_b, lru_wa, lru_ba, lru_wx, lru_bx, lru_lambda, w_out,
           ffn_w1, ffn_w3, ffn_w2, router_w, moe_w1, moe_w3, moe_w2, final_g):
    batch, seq, d = x_prompt.shape
    dec_batch, dec_seq, _ = x_sample.shape
    depth = w_in.shape[0]
    cw = conv_w.shape[2]
    lw = lru_lambda.shape[2]
    n_heads = attn_sink.shape[1]
    aw = n_heads * HEAD_DIM
    kw = N_KV_HEADS * HEAD_DIM
    past = cache_k.shape[2]
    n_experts = router_w.shape[2]
    assert cw == lw and dec_batch + 1 <= 8 and dec_seq // WINDOW_BLOCK >= 2
    geom = _Geom(batch, seq, dec_batch, dec_seq)

    cond = jnp.zeros((8, d), F32).at[0].set(c_ctx).at[1:1 + dec_batch].set(c)
    mods = _modulation(cond, w_mod, b_mod).reshape(depth, 8, 6, d)
    cos_t, sin_t = _rope_tables(dec_seq)
    cache_k = cache_k.reshape(dec_batch, depth, past, kw)
    cache_v = cache_v.reshape(dec_batch, depth, past, kw)
    zero_state = jnp.zeros((batch, 2, lw), F32)

    x = (x_prompt.reshape(geom.n_ctx, d), x_sample.reshape(geom.n_lat, d))
    new_k, new_v, new_lru = [], [], []
    y_prompt = y_sample = None
    for l in range(depth):
        sink_b = jnp.broadcast_to(attn_sink[l][:, None], (n_heads, 128))
        lru_p = (lru_conv_w[l], lru_conv_b[l].reshape(2, 1, lw),
                 jnp.concatenate([jnp.stack([_block_diag(lru_wa[l, dd]) for dd in range(2)]),
                                  jnp.stack([_block_diag(lru_wx[l, dd]) for dd in range(2)])], axis=2).astype(BF16),
                 jnp.concatenate([lru_ba[l], lru_bx[l]], axis=1).reshape(2, 1, 2 * lw),
                 lru_lambda[l].reshape(2, 1, lw))

        cbz, zb, q, k, v, lru_in = _stage_a(geom, x, mods, l, norm1_g[l], w_in[l].astype(BF16), cos_t, sin_t, cw, aw, kw)
        new_k.append(k[:geom.n_ctx].reshape(batch, seq, N_KV_HEADS, HEAD_DIM))
        new_v.append(v[:geom.n_ctx].reshape(batch, seq, N_KV_HEADS, HEAD_DIM))

        y_att = (_attention_ctx(geom, q, k, v, sink_b), _attention_lat(geom, q, k, v, cache_k, cache_v, l, sink_b))
        lru_ctx, h_ctx = _lru(lru_in, 0, batch, seq, zero_state, lru_p)
        lru_lat, _ = _lru(lru_in, geom.n_ctx, dec_batch, dec_seq, state_lru[:, l], lru_p)
        y_lru = (lru_ctx, lru_lat)
        new_lru.append(h_ctx)

        if l % 2 == 0:
            x, h2 = _stage_c(geom, x, cbz, zb, y_att, y_lru, mods, l, conv_w[l], w_out[l].astype(BF16), norm2_g[l],
                             None, n_experts)
            i = l // 2
            x = (_ffn(geom, x, h2, mods, l, ffn_w1[i].astype(BF16), ffn_w3[i].astype(BF16), ffn_w2[i].astype(BF16)),)
            if l == depth - 1:
                y_prompt = _final_norm(x[0], final_g, 0, geom.n_ctx)
                y_sample = _final_norm(x[0], final_g, geom.n_ctx, geom.n_lat)
        else:
            i = l // 2
            router_bf = jnp.zeros((d, 128), F32).at[:, :n_experts].set(router_w[i]).astype(BF16)
            x, h2p, route = _stage_c(geom, x, cbz, zb, y_att, y_lru, mods, l, conv_w[l], w_out[l].astype(BF16),
                                     norm2_g[l], router_bf, n_experts)
            pos, n_rows, zero_tiles, plan = _moe_plan(route, n_experts)
            pos_flat = pos.reshape(-1)
            xs = _moe_scatter(h2p, pos_flat, zero_tiles, n_rows)
            ys = _moe_gemm(xs, plan, moe_w1[i], moe_w3[i], moe_w2[i])
            if l == depth - 1:
                y_prompt = _moe_combine(geom, x, route, mods, l, final_g, ys, pos_flat, 0, geom.n_ctx)
                y_sample = _moe_combine(geom, x, route, mods, l, final_g, ys, pos_flat, geom.n_ctx, geom.n_lat)
            else:
                raise NotImplementedError("a routed layer is only supported as the last layer")

    return (y_prompt.reshape(batch, seq, d), y_sample.reshape(dec_batch, dec_seq, d),
            jnp.stack(new_k, axis=1), jnp.stack(new_v, axis=1), jnp.stack(new_lru, axis=1))
```

```python
import functools

import jax
import jax.numpy as jnp
from jax import lax
from jax.experimental import pallas as pl
from jax.experimental.pallas import tpu as pltpu

F32 = jnp.float32
BF16 = jnp.bfloat16

HEAD_DIM = 64
N_KV_HEADS = 2
GQA_GROUP = 4
GRID_W = 64
ROPE_BASE = 10000.0
WINDOW_BLOCK = 128
LRU_C = 8.0
LRU_SEGMENTS = 8
LRU_PITCH_PAD = 4
EPS = 1e-6
NEG = -1e30
LOG2E = 1.4426950408889634
TOP_K = 2

ROW_TILE = 512
MOE_ROW_TILE = 512
MOE_SUPER = 2048
MOE_FF_TILE = 512
VMEM_LIMIT = 52 * 1024 * 1024
MOE_GEMM_VMEM_LIMIT = 58 * 1024 * 1024


def _params(*sem):
    return pltpu.CompilerParams(dimension_semantics=sem, vmem_limit_bytes=VMEM_LIMIT)


def _mod_kernel(cond_ref, w_ref, b_ref, o_ref):
    c = cond_ref[...]
    s = c * jax.nn.sigmoid(c)
    o_ref[0] = jnp.dot(s.astype(BF16), w_ref[0].astype(BF16), preferred_element_type=F32) + b_ref[0]


def _modulation(cond, w_mod, b_mod):
    depth, d, d6 = w_mod.shape
    tn = d6 // 4
    return pl.pallas_call(
        _mod_kernel,
        grid=(depth, d6 // tn),
        in_specs=[pl.BlockSpec((8, d), lambda l, j: (0, 0)),
                  pl.BlockSpec((1, d, tn), lambda l, j: (l, 0, j)),
                  pl.BlockSpec((1, 1, tn), lambda l, j: (l, 0, j))],
        out_specs=pl.BlockSpec((1, 8, tn), lambda l, j: (l, 0, j)),
        out_shape=jax.ShapeDtypeStruct((depth, 8, d6), F32),
        compiler_params=_params("arbitrary", "arbitrary"),
        name="modulation",
    )(cond, w_mod, b_mod.reshape(depth, 1, d6))


class _Geom:
    def __init__(self, n_ctx_seq, ctx_len, n_lat_seq, lat_len):
        self.n_ctx_seq, self.ctx_len, self.n_lat_seq, self.lat_len = n_ctx_seq, ctx_len, n_lat_seq, lat_len
        self.n_ctx = n_ctx_seq * ctx_len
        self.n_lat = n_lat_seq * lat_len
        self.n_tok = self.n_ctx + self.n_lat
        assert self.n_ctx % ROW_TILE == 0 and lat_len % ROW_TILE == 0 and ROW_TILE % ctx_len == 0
        assert self.n_ctx % lat_len == 0 or self.n_ctx < lat_len
        self.ctx_tiles = self.n_ctx // ROW_TILE
        self.lat_tiles_per_seq = lat_len // ROW_TILE
        self.n_tiles = self.n_tok // ROW_TILE

    def mod_row(self, i):
        return jnp.where(i < self.ctx_tiles, 0, 1 + (i - self.ctx_tiles) // self.lat_tiles_per_seq)

    def rope_block(self, i):
        return jnp.where(i < self.ctx_tiles, 0, 1 + (i - self.ctx_tiles) % self.lat_tiles_per_seq)


def _rms_mod(x, g, shift, scale):
    y = x * lax.rsqrt(jnp.mean(x * x, axis=-1, keepdims=True) + EPS)
    return (y * g) * (1 + scale) + shift


def _rope(x, cos, sin):
    lane = lax.broadcasted_iota(jnp.int32, x.shape, 1)
    swapped = jnp.where((lane & 31) < 16, pltpu.roll(x, 128 - 16, 1), pltpu.roll(x, 16, 1))
    return x * cos + swapped * sin


def _stream_specs(geom, x_parts, d):
    tm = ROW_TILE
    if len(x_parts) == 1:
        return [pl.BlockSpec((tm, d), lambda i, *_: (i, 0))]
    ct = geom.ctx_tiles
    return [pl.BlockSpec((tm, d), lambda i, *_: (jnp.minimum(i, ct - 1), 0)),
            pl.BlockSpec((tm, d), lambda i, *_: (jnp.maximum(i - ct, 0), 0))]


def _stream_tile(x_refs, ctx_tiles):
    if len(x_refs) == 1:
        return x_refs[0][...]
    return jnp.where(pl.program_id(0) < ctx_tiles, x_refs[0][...], x_refs[1][...])


def _stage_a_kernel(*refs, n_x, ctx_tiles, cw, aw, kw):
    x_refs = refs[:n_x]
    mod_ref, g_ref, w_ref, cos_ref, sin_ref, cbz_ref, zb_ref, q_ref, k_ref, v_ref, lru_ref = refs[n_x:]
    m = mod_ref[0, 0]
    h = _rms_mod(_stream_tile(x_refs, ctx_tiles), g_ref[...], m[0:1], m[1:2]).astype(BF16)
    rows = h.shape[0]

    def proj(lo, hi):
        return jnp.dot(h, w_ref[:, lo:hi], preferred_element_type=F32)

    cb = proj(0, cw)
    z = proj(cw, 2 * cw) * proj(2 * cw, 3 * cw)
    cbz_ref[:, 0:cw] = cb
    cbz_ref[:, cw:2 * cw] = z
    zb_ref[0, 0:8] = z[0:8]
    zb_ref[0, 8:16] = z[rows - 8:rows]

    cos, sin = cos_ref[...], sin_ref[...]
    o = 3 * cw
    for c in range(aw // 128):
        qc = _rope(proj(o + c * 128, o + (c + 1) * 128), cos, sin)
        q_ref[:, c * 128:(c + 1) * 128] = (qc * (HEAD_DIM ** -0.5 * LOG2E)).astype(BF16)
    o += aw
    k_ref[...] = _rope(proj(o, o + kw), cos, sin)
    v_ref[...] = proj(o + kw, o + 2 * kw)
    o += 2 * kw
    lru_ref[...] = proj(o, o + 2 * cw)


def _stage_a(geom, x_parts, mods, l, norm_g, w_in_bf, cos_t, sin_t, cw, aw, kw):
    n, d = geom.n_tok, x_parts[0].shape[1]
    tm = ROW_TILE
    d_in = w_in_bf.shape[1]
    row = lambda i: (i, 0)
    out_shape = (jax.ShapeDtypeStruct((n, 2 * cw), F32),
                 jax.ShapeDtypeStruct((geom.n_tiles, 16, cw), F32),
                 jax.ShapeDtypeStruct((n, aw), BF16),
                 jax.ShapeDtypeStruct((n, kw), F32),
                 jax.ShapeDtypeStruct((n, kw), F32),
                 jax.ShapeDtypeStruct((n, 2 * cw), F32))
    return pl.pallas_call(
        functools.partial(_stage_a_kernel, n_x=len(x_parts), ctx_tiles=geom.ctx_tiles, cw=cw, aw=aw, kw=kw),
        grid=(geom.n_tiles,),
        in_specs=_stream_specs(geom, x_parts, d) + [
                  pl.BlockSpec((1, 1, 6, d), lambda i: (l, geom.mod_row(i), 0, 0)),
                  pl.BlockSpec((1, d), lambda i: (0, 0)),
                  pl.BlockSpec((d, d_in), lambda i: (0, 0)),
                  pl.BlockSpec((tm, 128), lambda i: (geom.rope_block(i), 0)),
                  pl.BlockSpec((tm, 128), lambda i: (geom.rope_block(i), 0))],
        out_specs=(pl.BlockSpec((tm, 2 * cw), row),
                   pl.BlockSpec((1, 16, cw), lambda i: (i, 0, 0)),
                   pl.BlockSpec((tm, aw), row),
                   pl.BlockSpec((tm, kw), row),
                   pl.BlockSpec((tm, kw), row),
                   pl.BlockSpec((tm, 2 * cw), row)),
        out_shape=out_shape,
        compiler_params=_params("arbitrary"),
        name=f"stage_a_l{l}",
    )(*x_parts, mods, norm_g.reshape(1, d), w_in_bf, cos_t, sin_t)


def _qk(q, k):
    return lax.dot_general(q, k, (((1,), (1,)), ((), ())), preferred_element_type=F32)


def _attn_kernel(*refs, windowed, nb):
    n_in = 10 if windowed else 4
    ins, rest = refs[:n_in], refs[n_in:]
    if windowed:
        q_ref, kl_ref, km_ref, kr_ref, vl_ref, vm_ref, vr_ref, kc_ref, vc_ref, sink_ref = ins
    else:
        q_ref, km_ref, vm_ref, sink_ref = ins
    o_ref, ka_ref, kb_ref, va_ref, vb_ref = rest
    tq = q_ref.shape[0]

    def fill(r0, k, v):
        n = k.shape[0]
        lane = lax.broadcasted_iota(jnp.int32, (n, 128), 1)
        low = lane < HEAD_DIM
        k_sw, v_sw = pltpu.roll(k, HEAD_DIM, 1), pltpu.roll(v, HEAD_DIM, 1)
        one_hi = jnp.where(lane == HEAD_DIM, 1.0, 0.0)
        one_lo = jnp.where(lane == 0, 1.0, 0.0)
        rows = pl.ds(r0, n)
        for g, (kg_lo, kg_hi, vg_lo, vg_hi) in enumerate(((k, k_sw, v, v_sw), (k_sw, k, v_sw, v))):
            ka_ref[g, rows, :] = jnp.where(low, kg_lo, 0.0).astype(BF16)
            kb_ref[g, rows, :] = jnp.where(low, 0.0, kg_hi).astype(BF16)
            va_ref[g, rows, :] = jnp.where(low, vg_lo, one_hi).astype(BF16)
            vb_ref[g, rows, :] = jnp.where(low, one_lo, vg_hi).astype(BF16)

    if windowed:
        j = pl.program_id(1)
        fill(0, kl_ref[...], vl_ref[...])
        fill(tq, km_ref[...], vm_ref[...])
        fill(2 * tq, kr_ref[...], vr_ref[...])

        @pl.when(j == 0)
        def _():
            fill(3 * tq, kc_ref[0, 0], vc_ref[0, 0])

        r = lax.broadcasted_iota(jnp.int32, (2 * tq, tq), 0) & (tq - 1)
        c = lax.broadcasted_iota(jnp.int32, (2 * tq, tq), 1)
        left_ok = jnp.logical_and(c >= r, j > 0)
        right_ok = jnp.logical_and(c <= r, j < nb - 1)
    else:
        fill(0, km_ref[...], vm_ref[...])

    sink_all = sink_ref[...] * LOG2E
    lane = lax.broadcasted_iota(jnp.int32, (2 * tq, 128), 1)
    for g in range(N_KV_HEADS):
        q2 = jnp.concatenate([q_ref[:, (2 * g) * 128:(2 * g + 1) * 128],
                              q_ref[:, (2 * g + 1) * 128:(2 * g + 2) * 128]], axis=0)
        res = []
        for odd, (k_ref, v_ref, den_lane) in enumerate(((ka_ref, va_ref, HEAD_DIM), (kb_ref, vb_ref, 0))):
            s = _qk(q2, k_ref[g])
            if windowed:
                s = jnp.concatenate([jnp.where(left_ok, s[:, 0:tq], NEG), s[:, tq:2 * tq],
                                     jnp.where(right_ok, s[:, 2 * tq:3 * tq], NEG), s[:, 3 * tq:]], axis=1)
            h0 = GQA_GROUP * g + odd
            sink_col = jnp.concatenate([jnp.broadcast_to(sink_all[h0:h0 + 1, 0:1], (tq, 1)),
                                        jnp.broadcast_to(sink_all[h0 + 2:h0 + 3, 0:1], (tq, 1))], axis=0)
            m = jnp.maximum(jnp.max(s, axis=-1, keepdims=True), sink_col)
            o = jnp.dot(jnp.exp2(s - m).astype(BF16), v_ref[g], preferred_element_type=F32)
            res.append(o / (o[:, den_lane:den_lane + 1] + jnp.exp2(sink_col - m)))
        merged = jnp.where(lane < HEAD_DIM, res[0], res[1]).astype(BF16)
        o_ref[:, (2 * g) * 128:(2 * g + 1) * 128] = merged[0:tq]
        o_ref[:, (2 * g + 1) * 128:(2 * g + 2) * 128] = merged[tq:2 * tq]


def _attn_scratch(n_keys):
    return [pltpu.VMEM((N_KV_HEADS, n_keys, 128), BF16)] * 4


def _attention_ctx(geom, q, k, v, sink_b):
    t, aw, kw = geom.ctx_len, q.shape[1], k.shape[1]
    assert kw == 128 and aw == 2 * N_KV_HEADS * 128
    blk = lambda b: (b, 0)
    return pl.pallas_call(
        functools.partial(_attn_kernel, windowed=False, nb=1),
        grid=(geom.n_ctx_seq,),
        in_specs=[pl.BlockSpec((t, aw), blk), pl.BlockSpec((t, kw), blk), pl.BlockSpec((t, kw), blk),
                  pl.BlockSpec(sink_b.shape, lambda b: (0, 0))],
        out_specs=pl.BlockSpec((t, aw), blk),
        out_shape=jax.ShapeDtypeStruct((geom.n_ctx, aw), BF16),
        scratch_shapes=_attn_scratch(t),
        compiler_params=_params("arbitrary"),
        name="attention_ctx",
    )(q, k, v, sink_b)


def _attention_lat(geom, q, k, v, cache_k, cache_v, l, sink_b):
    aw, kw = q.shape[1], k.shape[1]
    tq = WINDOW_BLOCK
    nb = geom.lat_len // tq
    off = geom.n_ctx // tq
    past = cache_k.shape[2]

    def at(delta):
        return lambda b, j: (off + b * nb + jnp.clip(j + delta, 0, nb - 1), 0)

    kv_spec = [pl.BlockSpec((tq, kw), at(-1)), pl.BlockSpec((tq, kw), at(0)), pl.BlockSpec((tq, kw), at(1))]
    cache_spec = pl.BlockSpec((1, 1, past, kw), lambda b, j: (b, l, 0, 0))
    return pl.pallas_call(
        functools.partial(_attn_kernel, windowed=True, nb=nb),
        grid=(geom.n_lat_seq, nb),
        in_specs=[pl.BlockSpec((tq, aw), at(0))] + kv_spec + kv_spec + [cache_spec, cache_spec,
                  pl.BlockSpec(sink_b.shape, lambda b, j: (0, 0))],
        out_specs=pl.BlockSpec((tq, aw), lambda b, j: (b * nb + j, 0)),
        out_shape=jax.ShapeDtypeStruct((geom.n_lat, aw), BF16),
        scratch_shapes=_attn_scratch(3 * tq + past),
        compiler_params=_params("arbitrary", "arbitrary"),
        name="attention_lat",
    )(q, k, k, k, v, v, v, cache_k, cache_v, sink_b)


def _log_sigmoid(x):
    return jnp.minimum(x, 0.0) - jnp.log(1.0 + jnp.exp(-jnp.abs(x)))


def _lru_kernel(x_ref, h0_ref, cw_ref, cb_ref, wg_ref, bg_ref, lam_ref, y_ref, ht_ref,
                xp_ref, af_ref, uf_ref, ab_ref, ub_ref, hin_ref, *, t, lw):
    seg = LRU_SEGMENTS
    pitch = t // seg + LRU_PITCH_PAD
    tg = min(t, 256)
    tc = min(t, 256)
    nh = lw // 128

    def put(ref, r0, rows, val):
        for hh in range(nh):
            ref[hh, pl.ds(r0, rows), :] = val[:, hh * 128:(hh + 1) * 128]

    def get(ref, r0, rows):
        return jnp.concatenate([ref[hh, pl.ds(r0, rows), :] for hh in range(nh)], axis=1)

    zeros8 = jnp.zeros((8, lw), F32)
    xp_ref[0:8] = zeros8
    xp_ref[t + 8:t + 16] = zeros8
    xp_ref[8:t + 8] = x_ref[:, 0:lw]
    n_pad = seg * pitch - t
    for hh in range(nh):
        for ref, fill in ((af_ref, 1.0), (ab_ref, 1.0), (uf_ref, 0.0), (ub_ref, 0.0)):
            ref[hh, t:t + n_pad, :] = jnp.full((n_pad, 128), fill, F32)

    dirs = ((af_ref, uf_ref), (ab_ref, ub_ref))
    log_sig = [_log_sigmoid(lam_ref[d]) for d in range(2)]
    taps = cw_ref.shape[1]

    def gate_chunk(c, carry):
        r0 = pl.multiple_of(c * tg, tg)
        win = xp_ref[pl.ds(r0, tg + 16), :]
        for d, (a_ref, u_ref) in enumerate(dirs):
            cw = cw_ref[d]
            xc = None
            for k in range(taps):
                s0 = 8 - (taps - 1) + k if d == 0 else 8 + k
                term = win[s0:s0 + tg] * cw[k:k + 1]
                xc = term if xc is None else xc + term
            xc = xc + cb_ref[d]
            gates = jnp.dot(xc.astype(BF16), wg_ref[d], preferred_element_type=F32) + bg_ref[d]
            r = 0.5 * jnp.tanh(0.5 * gates[:, 0:lw]) + 0.5
            i = 0.5 * jnp.tanh(0.5 * gates[:, lw:2 * lw]) + 0.5
            log_a = (LRU_C * r) * log_sig[d]
            a = jnp.exp(log_a)
            u = jnp.sqrt(-jnp.tanh(log_a) * (a * a + 1.0)) * (i * xc)
            put(a_ref, r0, tg, a)
            put(u_ref, r0, tg, u)
        return carry

    lax.fori_loop(0, t // tg, gate_chunk, 0)

    def scan_step(jj, carry):
        out = []
        for d, (a_ref, u_ref) in enumerate(dirs):
            j = jj if d == 0 else pitch - 1 - jj
            idx = pl.ds(j, seg, stride=pitch)
            for hh in range(nh):
                p, u = carry[len(out)], carry[len(out) + 1]
                a = a_ref[hh, idx, :]
                p = a * p
                u = a * u + u_ref[hh, idx, :]
                a_ref[hh, idx, :] = p
                u_ref[hh, idx, :] = u
                out += [p, u]
        return tuple(out)

    init = (jnp.ones((seg, 128), F32), jnp.zeros((seg, 128), F32)) * (2 * nh)
    ends = lax.fori_loop(0, pitch, scan_step, init, unroll=4)

    for d in range(2):
        p_end = jnp.concatenate(ends[2 * nh * d:2 * nh * (d + 1):2], axis=1)
        u_end = jnp.concatenate(ends[2 * nh * d + 1:2 * nh * (d + 1):2], axis=1)
        hcur = h0_ref[0, d:d + 1, :]
        for s in (range(seg) if d == 0 else range(seg - 1, -1, -1)):
            hin_ref[d, s:s + 1, :] = hcur
            hcur = p_end[s:s + 1] * hcur + u_end[s:s + 1]
        ht_ref[0, d:d + 1, :] = hcur

    def hin_rows(d, r0):
        row = r0 + lax.broadcasted_iota(jnp.int32, (tc, 1), 0)
        if tc <= pitch:
            s0 = r0 // pitch
            s1 = jnp.minimum(s0 + 1, seg - 1)
            return jnp.where(row < (s0 + 1) * pitch, hin_ref[d, pl.ds(s0, 1), :], hin_ref[d, pl.ds(s1, 1), :])
        acc = None
        for s in range(seg):
            term = jnp.where(jnp.logical_and(row >= s * pitch, row < (s + 1) * pitch), hin_ref[d, s:s + 1, :], 0.0)
            acc = term if acc is None else acc + term
        return acc

    def out_chunk(c, carry):
        r0 = pl.multiple_of(c * tc, tc)
        hf = get(af_ref, r0, tc) * hin_rows(0, r0) + get(uf_ref, r0, tc)
        hb = get(ab_ref, r0, tc) * hin_rows(1, r0) + get(ub_ref, r0, tc)
        lg = x_ref[pl.ds(r0, tc), lw:2 * lw]
        y_ref[pl.ds(r0, tc), :] = ((hf + hb) * jax.nn.gelu(lg)).astype(BF16)
        return carry

    lax.fori_loop(0, t // tc, out_chunk, 0)


def _lru(lru_in, row_off, n_seq, t, h0, lru_p):
    conv_w, conv_b, w_gate, b_gate, lam = lru_p
    lw = lru_in.shape[1] // 2
    assert row_off % t == 0 and t % (8 * LRU_SEGMENTS) == 0
    off = row_off // t
    whole = lambda a: pl.BlockSpec(a.shape, lambda b: (0,) * a.ndim)
    return pl.pallas_call(
        functools.partial(_lru_kernel, t=t, lw=lw),
        grid=(n_seq,),
        in_specs=[pl.BlockSpec((t, 2 * lw), lambda b: (off + b, 0)),
                  pl.BlockSpec((1, 2, lw), lambda b: (b, 0, 0)),
                  whole(conv_w), whole(conv_b), whole(w_gate), whole(b_gate), whole(lam)],
        out_specs=(pl.BlockSpec((t, lw), lambda b: (b, 0)),
                   pl.BlockSpec((1, 2, lw), lambda b: (b, 0, 0))),
        out_shape=(jax.ShapeDtypeStruct((n_seq * t, lw), BF16),
                   jax.ShapeDtypeStruct((n_seq, 2, lw), F32)),
        scratch_shapes=[pltpu.VMEM((t + 16, lw), F32)]
                       + [pltpu.VMEM((lw // 128, t + LRU_SEGMENTS * LRU_PITCH_PAD, 128), F32)] * 4
                       + [pltpu.VMEM((2, LRU_SEGMENTS, lw), F32)],
        compiler_params=_params("arbitrary"),
        name=f"rglru_t{t}",
    )(lru_in, h0, conv_w, conv_b, w_gate, b_gate, lam)


def _route_top2(h2b, router_ref, n_experts):
    logits = jnp.dot(h2b, router_ref[...], preferred_element_type=F32)
    lane = lax.broadcasted_iota(jnp.int32, logits.shape, 1).astype(F32)
    lg = jnp.where(lane < n_experts, logits, -jnp.inf)
    v1 = jnp.max(lg, axis=-1, keepdims=True)
    e1 = jnp.min(jnp.where(lg == v1, lane, 1e9), axis=-1, keepdims=True)
    lg2 = jnp.where(lane == e1, -jnp.inf, lg)
    v2 = jnp.max(lg2, axis=-1, keepdims=True)
    e2 = jnp.min(jnp.where(lg2 == v2, lane, 1e9), axis=-1, keepdims=True)
    ex = jnp.exp(v2 - v1)
    den = 1.0 + ex
    g1 = 1.0 / den
    g2 = ex / den
    return jnp.where(lane == 0, e1, jnp.where(lane == 1, e2, jnp.where(lane == 2, g1, jnp.where(lane == 3, g2, 0.0))))


def _stage_c_kernel(*refs, n_x, geom, cw, aw, routed, n_experts):
    x_refs, att_refs, lru_refs, refs = refs[:n_x], refs[n_x:n_x + 2], refs[n_x + 2:n_x + 4], refs[n_x + 4:]
    if routed:
        (cbz_ref, zprev_ref, znext_ref, mod_ref, convw_ref, wo_ref, g_ref, router_ref,
         xo_ref, h2_ref, route_ref) = refs
    else:
        (cbz_ref, zprev_ref, znext_ref, mod_ref, convw_ref, wo_ref, g_ref, xo_ref, h2_ref) = refs
    i = pl.program_id(0)
    tm = cbz_ref.shape[0]
    cb = cbz_ref[:, 0:cw]
    z = cbz_ref[:, cw:2 * cw]
    row = lax.broadcasted_iota(jnp.int32, (tm, 1), 0)
    seq_len = jnp.where(i < geom.ctx_tiles, geom.ctx_len, geom.lat_len)
    tpos = (i * tm + row) & (seq_len - 1)
    z_dn = jnp.where(row == 0, zprev_ref[0, 15:16], pltpu.roll(z, 1, 0))
    z_up = jnp.where(row == tm - 1, znext_ref[0, 0:1], pltpu.roll(z, tm - 1, 0))
    z_dn = jnp.where(tpos == 0, 0.0, z_dn)
    z_up = jnp.where(tpos == seq_len - 1, 0.0, z_up)
    cwt = convw_ref[...]
    y_conv = cb * (z_dn * cwt[0:1] + z * cwt[1:2] + z_up * cwt[2:3])

    mix = (jnp.dot(y_conv.astype(BF16), wo_ref[0:cw], preferred_element_type=F32)
           + jnp.dot(_stream_tile(att_refs, geom.ctx_tiles), wo_ref[cw:cw + aw], preferred_element_type=F32)
           + jnp.dot(_stream_tile(lru_refs, geom.ctx_tiles), wo_ref[cw + aw:2 * cw + aw],
                     preferred_element_type=F32))
    m = mod_ref[0, 0]
    xn = _stream_tile(x_refs, geom.ctx_tiles) + m[2:3] * mix
    xo_ref[...] = xn
    h2 = _rms_mod(xn, g_ref[...], m[3:4], m[4:5])
    if routed:
        h2_ref[...] = h2
        route_ref[...] = _route_top2(h2.astype(BF16), router_ref, n_experts)
    else:
        h2_ref[...] = h2.astype(BF16)


def _stage_c(geom, x_parts, cbz, zb, y_att, y_lru, mods, l, conv_w, w_out_bf, norm_g, router_bf, n_experts):
    n, d = geom.n_tok, x_parts[0].shape[1]
    tm = ROW_TILE
    cw, aw = conv_w.shape[1], y_att[0].shape[1]
    routed = router_bf is not None
    row = lambda i: (i, 0)
    last = geom.n_tiles - 1
    in_specs = _stream_specs(geom, x_parts, d) + _stream_specs(geom, y_att, aw) + _stream_specs(geom, y_lru, cw) + [
                pl.BlockSpec((tm, 2 * cw), row),
                pl.BlockSpec((1, 16, cw), lambda i: (jnp.maximum(i - 1, 0), 0, 0)),
                pl.BlockSpec((1, 16, cw), lambda i: (jnp.minimum(i + 1, last), 0, 0)),
                pl.BlockSpec((1, 1, 6, d), lambda i: (l, geom.mod_row(i), 0, 0)),
                pl.BlockSpec(conv_w.shape, lambda i: (0, 0)),
                pl.BlockSpec(w_out_bf.shape, lambda i: (0, 0)),
                pl.BlockSpec((1, d), lambda i: (0, 0))]
    args = [*x_parts, *y_att, *y_lru, cbz, zb, zb, mods, conv_w, w_out_bf, norm_g.reshape(1, d)]
    out_specs = [pl.BlockSpec((tm, d), row)]
    out_shape = [jax.ShapeDtypeStruct((n, d), F32)]
    if routed:
        in_specs.append(pl.BlockSpec(router_bf.shape, lambda i: (0, 0)))
        args.append(router_bf)
        out_specs += [pl.BlockSpec((tm, d), row), pl.BlockSpec((tm, 128), row)]
        out_shape += [jax.ShapeDtypeStruct((n, d), F32), jax.ShapeDtypeStruct((n, 128), F32)]
    else:
        out_specs.append(pl.BlockSpec((tm, d), row))
        out_shape.append(jax.ShapeDtypeStruct((n, d), BF16))
    return pl.pallas_call(
        functools.partial(_stage_c_kernel, n_x=len(x_parts), geom=geom, cw=cw, aw=aw, routed=routed,
                          n_experts=n_experts),
        grid=(geom.n_tiles,),
        in_specs=in_specs, out_specs=tuple(out_specs), out_shape=tuple(out_shape),
        compiler_params=_params("arbitrary"),
        name=f"stage_c_l{l}",
    )(*args)


def _silu(x):
    return x * jax.nn.sigmoid(x)


def _ffn_kernel(x_ref, h_ref, mod_ref, w1_ref, w3_ref, w2_ref, o_ref, *, fc):
    h = h_ref[...]
    ff = w1_ref.shape[1]
    acc = None
    for c in range(ff // fc):
        sl = slice(c * fc, (c + 1) * fc)
        a = _silu(jnp.dot(h, w1_ref[:, sl], preferred_element_type=F32)) * jnp.dot(
            h, w3_ref[:, sl], preferred_element_type=F32)
        part = jnp.dot(a.astype(BF16), w2_ref[sl, :], preferred_element_type=F32)
        acc = part if acc is None else acc + part
    o_ref[...] = x_ref[...] + mod_ref[0, 0][5:6] * acc


def _ffn(geom, x, h2, mods, l, w1, w3, w2):
    n, d = x.shape
    tm = ROW_TILE
    ff = w1.shape[1]
    fc = 256
    assert ff % fc == 0
    row = lambda i: (i, 0)
    const = lambda a: pl.BlockSpec(a.shape, lambda i: (0, 0), pipeline_mode=pl.Buffered(1))
    return pl.pallas_call(
        functools.partial(_ffn_kernel, fc=fc),
        grid=(geom.n_tiles,),
        in_specs=[pl.BlockSpec((tm, d), row), pl.BlockSpec((tm, d), row),
                  pl.BlockSpec((1, 1, 6, d), lambda i: (l, geom.mod_row(i), 0, 0)),
                  const(w1), const(w3), const(w2)],
        out_specs=pl.BlockSpec((tm, d), row),
        out_shape=jax.ShapeDtypeStruct((n, d), F32),
        compiler_params=_params("arbitrary"),
        name=f"ffn_l{l}",
    )(x, h2, mods, w1, w3, w2)


def _moe_plan(route, n_experts):
    n = route.shape[0]
    tmg, sup = MOE_ROW_TILE, MOE_SUPER
    i32 = jnp.int32
    experts = route[:, 0:TOP_K].astype(i32)
    onehot = (experts[:, :, None] == jnp.arange(n_experts, dtype=i32)[None, None, :]).astype(i32)
    sel = jnp.sum(onehot, axis=1)
    incl = jnp.cumsum(sel, axis=0)
    rank = incl - sel
    counts = incl[-1]
    tiles = (counts + tmg - 1) // tmg
    tile_end = jnp.cumsum(tiles)
    tile_off = tile_end - tiles
    pos = jnp.sum(onehot * ((tile_off * tmg)[None, :] + rank)[:, None, :], axis=2)

    n_rows = (TOP_K * n + n_experts * (tmg - 1)) // tmg * tmg
    n_rows = (n_rows + sup - 1) // sup * sup
    n_super = n_rows // sup
    per = sup // tmg
    n_work = n_super + n_experts - 1
    used = tile_end[-1]
    last_sup = (used - 1) // per
    end_ext = jnp.where((tile_end == used) & (tiles > 0), (last_sup + 1) * per, tile_end)
    s0 = (jnp.arange(n_super, dtype=i32) * per)[:, None]
    lo = jnp.clip(tile_off[None, :] - s0, 0, per).reshape(-1)
    hi = jnp.clip(end_ext[None, :] - s0, 0, per).reshape(-1)
    live = ((hi > lo) & jnp.tile(tiles > 0, n_super)).astype(i32)
    slot = jnp.cumsum(live) - live
    n_live = jnp.sum(live)
    w = jnp.arange(n_work, dtype=i32)
    hit = (slot[None, :] == jnp.minimum(w, n_live - 1)[:, None]) & (live[None, :] > 0)

    def pick(v):
        return jnp.sum(jnp.where(hit, v[None, :], 0), axis=1).astype(i32)

    pair = jnp.arange(n_super * n_experts, dtype=i32)
    busy = w < n_live
    n_tiles = n_rows // tmg
    tail = jnp.arange(n_tiles - (TOP_K * n + tmg - 1) // tmg, dtype=i32)
    zero_tiles = jnp.concatenate([jnp.maximum(tile_end - 1, 0), jnp.minimum(used + tail, n_tiles - 1)]).astype(i32)
    return pos.astype(i32), n_rows, zero_tiles, (pick(pair // n_experts), pick(pair % n_experts),
                                                 jnp.where(busy, pick(lo), 0), jnp.where(busy, pick(hi), 0))


def _moe_scatter_kernel(pos_ref, zt_ref, h_ref, xs_ref, zero_ref, sem, zsem, *, tile_off):
    i = pl.program_id(0)
    tm = h_ref.shape[0]
    base = (tile_off + i) * tm * TOP_K

    @pl.when(i == 0)
    def _():
        zero_ref[...] = jnp.zeros(zero_ref.shape, F32)
        for t in range(zt_ref.shape[0]):
            z = pltpu.make_async_copy(zero_ref, xs_ref.at[pl.ds(pl.multiple_of(zt_ref[t] * tm, tm), tm)], zsem)
            z.start()
            z.wait()

    def copy(r, k):
        return pltpu.make_async_copy(h_ref.at[pl.ds(r, 1)], xs_ref.at[pl.ds(pos_ref[base + r * TOP_K + k], 1)], sem)

    def start(r, c):
        for k in range(TOP_K):
            copy(r, k).start(priority=k)
        return c

    def wait(r, c):
        for k in range(TOP_K):
            copy(r, k).wait()
        return c

    lax.fori_loop(0, tm, start, 0, unroll=8)
    lax.fori_loop(0, tm, wait, 0, unroll=8)


def _moe_scatter(h2p, pos_flat, zero_tiles, n_rows):
    n, w = h2p.shape
    tm = ROW_TILE
    assert tm == MOE_ROW_TILE
    return pl.pallas_call(
        functools.partial(_moe_scatter_kernel, tile_off=0),
        grid_spec=pltpu.PrefetchScalarGridSpec(
            num_scalar_prefetch=2,
            grid=(n // tm,),
            in_specs=[pl.BlockSpec((tm, w), lambda i, pos, zt: (i, 0))],
            out_specs=pl.BlockSpec(memory_space=pl.ANY),
            scratch_shapes=[pltpu.VMEM((tm, w), F32), pltpu.SemaphoreType.DMA(()), pltpu.SemaphoreType.DMA(())]),
        out_shape=jax.ShapeDtypeStruct((n_rows, w), F32),
        compiler_params=_params("arbitrary"),
        name="moe_scatter",
    )(pos_flat, zero_tiles, h2p)


def _moe_gemm_kernel(wsup_ref, wexp_ref, lo_ref, hi_ref, xs_ref, w1_ref, w3_ref, w2_ref, o_ref,
                     w1b_ref, w3b_ref, w2b_ref):
    w = pl.program_id(0)
    j = pl.program_id(1)
    tmg = MOE_ROW_TILE
    lo, hi = lo_ref[w], hi_ref[w]
    fc = 256

    @pl.when(hi > lo)
    def _():
        w1b_ref[...] = w1_ref[0].astype(BF16)
        w3b_ref[...] = w3_ref[0].astype(BF16)
        w2b_ref[...] = w2_ref[0].astype(BF16)

        def sub(s, carry, first):
            r0 = pl.multiple_of(s * tmg, tmg)
            xb = xs_ref[pl.ds(r0, tmg), :].astype(BF16)
            part = None
            for c in range(w1b_ref.shape[1] // fc):
                sl = slice(c * fc, (c + 1) * fc)
                h1 = jnp.dot(xb, w1b_ref[:, sl], preferred_element_type=F32)
                h3 = jnp.dot(xb, w3b_ref[:, sl], preferred_element_type=F32)
                pc = jnp.dot((_silu(h1) * h3).astype(BF16), w2b_ref[sl, :], preferred_element_type=F32)
                part = pc if part is None else part + pc
            if first:
                o_ref[pl.ds(r0, tmg), :] = part
            else:
                o_ref[pl.ds(r0, tmg), :] += part
            return carry

        @pl.when(j == 0)
        def _():
            lax.fori_loop(lo, hi, functools.partial(sub, first=True), 0)

        @pl.when(j > 0)
        def _():
            lax.fori_loop(lo, hi, functools.partial(sub, first=False), 0)


def _moe_gemm(xs, plan, w1, w3, w2):
    n_rows, wp = xs.shape
    n_exp, d, ff = w1.shape
    sup, tf = MOE_SUPER, MOE_FF_TILE
    w_sup, w_exp, lo, hi = plan
    n_work = w_sup.shape[0]
    return pl.pallas_call(
        _moe_gemm_kernel,
        grid_spec=pltpu.PrefetchScalarGridSpec(
            num_scalar_prefetch=4,
            grid=(n_work, ff // tf),
            in_specs=[pl.BlockSpec((sup, wp), lambda w, j, ws, we, l_, h_: (ws[w], 0)),
                      pl.BlockSpec((1, d, tf), lambda w, j, ws, we, l_, h_: (we[w], 0, j)),
                      pl.BlockSpec((1, d, tf), lambda w, j, ws, we, l_, h_: (we[w], 0, j)),
                      pl.BlockSpec((1, tf, d), lambda w, j, ws, we, l_, h_: (we[w], j, 0))],
            out_specs=pl.BlockSpec((sup, d), lambda w, j, ws, we, l_, h_: (ws[w], 0)),
            scratch_shapes=[pltpu.VMEM((d, tf), BF16), pltpu.VMEM((d, tf), BF16), pltpu.VMEM((tf, d), BF16)]),
        out_shape=jax.ShapeDtypeStruct((n_rows, d), F32),
        compiler_params=pltpu.CompilerParams(dimension_semantics=("arbitrary", "arbitrary"),
                                             vmem_limit_bytes=MOE_GEMM_VMEM_LIMIT),
        name="moe_gemm",
    )(w_sup, w_exp, lo, hi, xs, w1, w3, w2)


def _moe_combine_kernel(pos_ref, x_ref, route_ref, mod_ref, g_ref, ys_ref, o_ref, buf_ref, sem,
                        *, tile_off, n_tiles):
    i = pl.program_id(0)
    tm = x_ref.shape[0]

    def copy(tile, slot, r, k):
        base = (tile_off + tile) * tm * TOP_K
        return pltpu.make_async_copy(ys_ref.at[pl.ds(pos_ref[base + r * TOP_K + k], 1)],
                                     buf_ref.at[slot, k, pl.ds(r, 1)], sem.at[slot])

    def issue(tile, slot):
        def start(r, c):
            for k in range(TOP_K):
                copy(tile, slot, r, k).start(priority=k)
            return c

        lax.fori_loop(0, tm, start, 0, unroll=8)

    @pl.when(i == 0)
    def _():
        issue(0, 0)

    @pl.when(i + 1 < n_tiles)
    def _():
        issue(i + 1, (i + 1) % 2)

    slot = i % 2

    def wait(r, c):
        for k in range(TOP_K):
            copy(i, slot, r, k).wait()
        return c

    lax.fori_loop(0, tm, wait, 0, unroll=8)
    route = route_ref[...]
    moe = route[:, 2:3] * buf_ref[slot, 0] + route[:, 3:4] * buf_ref[slot, 1]
    xn = x_ref[...] + mod_ref[0, 0][5:6] * moe
    y = xn * lax.rsqrt(jnp.mean(xn * xn, axis=-1, keepdims=True) + EPS)
    o_ref[...] = y * g_ref[...]


def _moe_combine(geom, x, route, mods, l, final_g, ys, pos_flat, row_off, n_out):
    d = x.shape[1]
    tm = ROW_TILE
    t0 = row_off // tm
    return pl.pallas_call(
        functools.partial(_moe_combine_kernel, tile_off=t0, n_tiles=n_out // tm),
        grid_spec=pltpu.PrefetchScalarGridSpec(
            num_scalar_prefetch=1,
            grid=(n_out // tm,),
            in_specs=[pl.BlockSpec((tm, d), lambda i, pos: (t0 + i, 0)),
                      pl.BlockSpec((tm, 128), lambda i, pos: (t0 + i, 0)),
                      pl.BlockSpec((1, 1, 6, d), lambda i, pos: (l, geom.mod_row(t0 + i), 0, 0)),
                      pl.BlockSpec((1, d), lambda i, pos: (0, 0)),
                      pl.BlockSpec(memory_space=pl.ANY)],
            out_specs=pl.BlockSpec((tm, d), lambda i, pos: (i, 0)),
            scratch_shapes=[pltpu.VMEM((2, TOP_K, tm, d), F32), pltpu.SemaphoreType.DMA((2,))]),
        out_shape=jax.ShapeDtypeStruct((n_out, d), F32),
        compiler_params=_params("arbitrary"),
        name="moe_combine",
    )(pos_flat, x, route, mods, final_g.reshape(1, d), ys)


def _final_norm_kernel(x_ref, g_ref, o_ref):
    x = x_ref[...]
    o_ref[...] = (x * lax.rsqrt(jnp.mean(x * x, axis=-1, keepdims=True) + EPS)) * g_ref[...]


def _final_norm(x, final_g, row_off, n_out):
    d = x.shape[1]
    tm = ROW_TILE
    t0 = row_off // tm
    return pl.pallas_call(
        _final_norm_kernel,
        grid=(n_out // tm,),
        in_specs=[pl.BlockSpec((tm, d), lambda i: (t0 + i, 0)), pl.BlockSpec((1, d), lambda i: (0, 0))],
        out_specs=pl.BlockSpec((tm, d), lambda i: (i, 0)),
        out_shape=jax.ShapeDtypeStruct((n_out, d), F32),
        compiler_params=_params("arbitrary"),
        name="final_norm",
    )(x, final_g.reshape(1, d))


def _rope_tables(t):
    pairs = HEAD_DIM // 4
    pos = jnp.arange(t)
    inv = 1.0 / (ROPE_BASE ** (jnp.arange(pairs, dtype=F32) / pairs))
    ar = (pos // GRID_W).astype(F32)[:, None] * inv
    ac = (pos % GRID_W).astype(F32)[:, None] * inv
    cos = jnp.concatenate([jnp.cos(ar), jnp.cos(ar), jnp.cos(ac), jnp.cos(ac)], axis=1)
    sin = jnp.concatenate([-jnp.sin(ar), jnp.sin(ar), -jnp.sin(ac), jnp.sin(ac)], axis=1)
    reps = 128 // HEAD_DIM
    cos = jnp.concatenate([jnp.ones((ROW_TILE, 128), F32), jnp.tile(cos, (1, reps))], axis=0)
    sin = jnp.concatenate([jnp.zeros((ROW_TILE, 128), F32), jnp.tile(sin, (1, reps))], axis=0)
    return cos, sin


def _block_diag(w):
    n, b, _ = w.shape
    eye = jnp.eye(n, dtype=w.dtype)
    return (eye[:, None, :, None] * w[:, :, None, :]).reshape(n * b, n * b)


def kernel(x_prompt, x_sample, cache_k, cache_v, state_lru, c, c_ctx, norm1_g, norm2_g, w_mod, b_mod, w_in,
           conv_w, attn_sink, lru_conv_w, lru_conv_b, lru_wa, lru_ba, lru_wx, lru_bx, lru_lambda, w_out,
           ffn_w1, ffn_w3, ffn_w2, router_w, moe_w1, moe_w3, moe_w2, final_g):
    batch, seq, d = x_prompt.shape
    dec_batch, dec_seq, _ = x_sample.shape
    depth = w_in.shape[0]
    cw = conv_w.shape[2]
    lw = lru_lambda.shape[2]
    n_heads = attn_sink.shape[1]
    aw = n_heads * HEAD_DIM
    kw = N_KV_HEADS * HEAD_DIM
    past = cache_k.shape[2]
    n_experts = router_w.shape[2]
    assert cw == lw and dec_batch + 1 <= 8 and dec_seq // WINDOW_BLOCK >= 2
    geom = _Geom(batch, seq, dec_batch, dec_seq)

    cond = jnp.zeros((8, d), F32).at[0].set(c_ctx).at[1:1 + dec_batch].set(c)
    mods = _modulation(cond, w_mod, b_mod).reshape(depth, 8, 6, d)
    cos_t, sin_t = _rope_tables(dec_seq)
    cache_k = cache_k.reshape(dec_batch, depth, past, kw)
    cache_v = cache_v.reshape(dec_batch, depth, past, kw)
    zero_state = jnp.zeros((batch, 2, lw), F32)

    x = (x_prompt.reshape(geom.n_ctx, d), x_sample.reshape(geom.n_lat, d))
    new_k, new_v, new_lru = [], [], []
    y_prompt = y_sample = None
    for l in range(depth):
        sink_b = jnp.broadcast_to(attn_sink[l][:, None], (n_heads, 128))
        lru_p = (lru_conv_w[l], lru_conv_b[l].reshape(2, 1, lw),
                 jnp.concatenate([jnp.stack([_block_diag(lru_wa[l, dd]) for dd in range(2)]),
                                  jnp.stack([_block_diag(lru_wx[l, dd]) for dd in range(2)])], axis=2).astype(BF16),
                 jnp.concatenate([lru_ba[l], lru_bx[l]], axis=1).reshape(2, 1, 2 * lw),
                 lru_lambda[l].reshape(2, 1, lw))

        cbz, zb, q, k, v, lru_in = _stage_a(geom, x, mods, l, norm1_g[l], w_in[l].astype(BF16), cos_t, sin_t, cw, aw, kw)
        new_k.append(k[:geom.n_ctx].reshape(batch, seq, N_KV_HEADS, HEAD_DIM))
        new_v.append(v[:geom.n_ctx].reshape(batch, seq, N_KV_HEADS, HEAD_DIM))

        y_att = (_attention_ctx(geom, q, k, v, sink_b), _attention_lat(geom, q, k, v, cache_k, cache_v, l, sink_b))
        lru_ctx, h_ctx = _lru(lru_in, 0, batch, seq, zero_state, lru_p)
        lru_lat, _ = _lru(lru_in, geom.n_ctx, dec_batch, dec_seq, state_lru[:, l], lru_p)
        y_lru = (lru_ctx, lru_lat)
        new_lru.append(h_ctx)

        if l % 2 == 0:
            x, h2 = _stage_c(geom, x, cbz, zb, y_att, y_lru, mods, l, conv_w[l], w_out[l].astype(BF16), norm2_g[l],
                             None, n_experts)
            i = l // 2
            x = (_ffn(geom, x, h2, mods, l, ffn_w1[i].astype(BF16), ffn_w3[i].astype(BF16), ffn_w2[i].astype(BF16)),)
            if l == depth - 1:
                y_prompt = _final_norm(x[0], final_g, 0, geom.n_ctx)
                y_sample = _final_norm(x[0], final_g, geom.n_ctx, geom.n_lat)
        else:
            i = l // 2
            router_bf = jnp.zeros((d, 128), F32).at[:, :n_experts].set(router_w[i]).astype(BF16)
            x, h2p, route = _stage_c(geom, x, cbz, zb, y_att, y_lru, mods, l, conv_w[l], w_out[l].astype(BF16),
                                     norm2_g[l], router_bf, n_experts)
            pos, n_rows, zero_tiles, plan = _moe_plan(route, n_experts)
            pos_flat = pos.reshape(-1)
            xs = _moe_scatter(h2p, pos_flat, zero_tiles, n_rows)
            ys = _moe_gemm(xs, plan, moe_w1[i], moe_w3[i], moe_w2[i])
            if l == depth - 1:
                y_prompt = _moe_combine(geom, x, route, mods, l, final_g, ys, pos_flat, 0, geom.n_ctx)
                y_sample = _moe_combine(geom, x, route, mods, l, final_g, ys, pos_flat, geom.n_ctx, geom.n_lat)
            else:
                raise NotImplementedError("a routed layer is only supported as the last layer")

    return (y_prompt.reshape(batch, seq, d), y_sample.reshape(dec_batch, dec_seq, d),
            jnp.stack(new_k, axis=1), jnp.stack(new_v, axis=1), jnp.stack(new_lru, axis=1))
```
